```python
import math
import jax, jax.numpy as jnp
from jax import lax
import numpy as np

D_MODEL = 1024
BATCH = 8
SEQ = 4096
DEPTH = 1

GRID_W = 64
CTX_LEN = 256
N_HEADS = 8
QK_NOPE_DIM = 64
QK_ROPE_DIM = 32
V_HEAD_DIM = 64
Q_LORA_RANK = 256
KV_LORA_RANK = 128
MLA_WIDTH = N_HEADS * V_HEAD_DIM
CONV_WIDTH = D_MODEL - MLA_WIDTH
CONV_K = 3
D_FF = 4 * D_MODEL
ROPE_THETA = 10000.0
ROPE_AXIS_DIM = QK_ROPE_DIM // 2
Q_BLOCK = 128
EPS = 1e-6
MLA_IN = Q_LORA_RANK + KV_LORA_RANK + QK_ROPE_DIM
IN_COLS = MLA_IN + 3 * CONV_WIDTH
QK_DIM = QK_NOPE_DIM + QK_ROPE_DIM
ATTN_SCALE = 1.0 / math.sqrt(QK_DIM)

kernel_name = 'hybrid_mla_shortconv_dit_layer'


def rmsnorm(x):
    xf = x.astype(jnp.float32)
    y = xf * lax.rsqrt(jnp.mean(xf * xf, axis=-1, keepdims=True) + EPS)
    return y.astype(x.dtype)


def modulate(x, shift, scale):
    return rmsnorm(x) * (1 + scale) + shift


def adaln(cvec, w_mod, b_mod):
    m = jax.nn.silu(cvec) @ w_mod + b_mod
    return jnp.split(m, 6, axis=-1)


def rope_tables(rows):
    row = jnp.broadcast_to(jnp.arange(rows)[:, None], (rows, GRID_W)).reshape(-1)
    col = jnp.broadcast_to(jnp.arange(GRID_W)[None, :], (rows, GRID_W)).reshape(-1)
    freqs = ROPE_THETA ** (-jnp.arange(0, ROPE_AXIS_DIM, 2, dtype=jnp.float32) / ROPE_AXIS_DIM)
    ang = jnp.stack([row.astype(jnp.float32)[:, None] * freqs,
                     col.astype(jnp.float32)[:, None] * freqs], axis=1)
    ang = ang[:, None]
    return jnp.cos(ang), jnp.sin(ang)


def apply_rope(x, cos, sin):
    xs = x.reshape(x.shape[:-1] + (2, 2, ROPE_AXIS_DIM // 2))
    x1, x2 = xs[..., 0, :], xs[..., 1, :]
    cos = cos.astype(x.dtype)
    sin = sin.astype(x.dtype)
    out = jnp.stack([x1 * cos - x2 * sin, x2 * cos + x1 * sin], axis=-2)
    return out.reshape(x.shape)


def mla_q(z, q_g, w_uq, cos, sin):
    cq = rmsnorm(z[..., :Q_LORA_RANK]) * q_g
    q = (cq @ w_uq).reshape(z.shape[:-1] + (N_HEADS, QK_DIM))
    q_nope, q_rope = q[..., :QK_NOPE_DIM], q[..., QK_NOPE_DIM:]
    if cos is not None:
        q_rope = apply_rope(q_rope, cos, sin)
    return jnp.concatenate([q_nope, q_rope], axis=-1)


def mla_kv(z, kv_g, w_ukv, cos, sin):
    ckv = rmsnorm(z[..., Q_LORA_RANK:Q_LORA_RANK + KV_LORA_RANK]) * kv_g
    k_rope = z[..., Q_LORA_RANK + KV_LORA_RANK:MLA_IN][..., None, :]
    kv = (ckv @ w_ukv).reshape(z.shape[:-1] + (N_HEADS, QK_NOPE_DIM + V_HEAD_DIM))
    k_nope, v = kv[..., :QK_NOPE_DIM], kv[..., QK_NOPE_DIM:]
    if cos is not None:
        k_rope = apply_rope(k_rope, cos, sin)
    k_rope = jnp.broadcast_to(k_rope, k_nope.shape[:-1] + (QK_ROPE_DIM,))
    return jnp.concatenate([k_nope, k_rope], axis=-1), v


def attention_dense(q, k, v):
    s = jnp.einsum('bqhd,bkhd->bhqk', q, k).astype(jnp.float32) * ATTN_SCALE
    p = jax.nn.softmax(s, axis=-1).astype(v.dtype)
    o = jnp.einsum('bhqk,bkhd->bqhd', p, v)
    return o.reshape(o.shape[:2] + (N_HEADS * V_HEAD_DIM,))


def attention_blocked(q, k, v):
    b, s = q.shape[0], q.shape[1]
    nblk = s // Q_BLOCK
    qb = q.reshape(b, nblk, Q_BLOCK, N_HEADS, QK_DIM).swapaxes(0, 1)
    o = lax.map(lambda qq: attention_dense(qq, k, v), qb)
    return o.swapaxes(0, 1).reshape(b, s, N_HEADS * V_HEAD_DIM)


def short_conv(z, conv_w):
    gb, gc, xin = jnp.split(z[..., MLA_IN:], 3, axis=-1)
    u = gc * xin
    n = u.shape[1]
    up = jnp.pad(u, ((0, 0), (1, 1), (0, 0)))
    y = conv_w[0] * up[:, :n] + conv_w[1] * up[:, 1:n + 1] + conv_w[2] * up[:, 2:n + 2]
    return gb * y


def sq_relu_mlp(h, w1, w2):
    return jnp.square(jax.nn.relu(h @ w1)) @ w2


def setup_inputs(seed: int = 0) -> dict:
    key = jax.random.key(seed)
    ks = jax.random.split(key, 16)
    f32 = jnp.float32
    n = lambda k, shape, s: jax.random.normal(k, shape, f32) * s
    return {
        'x': n(ks[0], (BATCH, SEQ, D_MODEL), 1.0),
        'c': n(ks[1], (BATCH, D_MODEL), 1.0),
        'ctx': n(ks[2], (BATCH, CTX_LEN, D_MODEL), 1.0),
        'c_ctx': n(ks[3], (D_MODEL,), 1.0),
        'w_mod': n(ks[4], (DEPTH, D_MODEL, 6 * D_MODEL), D_MODEL ** -0.5),
        'b_mod': n(ks[5], (DEPTH, 6 * D_MODEL), 0.02),
        'w_in': n(ks[6], (DEPTH, D_MODEL, IN_COLS), D_MODEL ** -0.5),
        'q_norm_g': 1.0 + n(ks[7], (DEPTH, Q_LORA_RANK), 0.1),
        'w_uq': n(ks[8], (DEPTH, Q_LORA_RANK, N_HEADS * QK_DIM), Q_LORA_RANK ** -0.5),
        'kv_norm_g': 1.0 + n(ks[9], (DEPTH, KV_LORA_RANK), 0.1),
        'w_ukv': n(ks[10], (DEPTH, KV_LORA_RANK, N_HEADS * (QK_NOPE_DIM + V_HEAD_DIM)), KV_LORA_RANK ** -0.5),
        'conv_w': n(ks[11], (DEPTH, CONV_K, CONV_WIDTH), CONV_K ** -0.5),
        'w_out': n(ks[12], (DEPTH, D_MODEL, D_MODEL), D_MODEL ** -0.5),
        'w_mlp1': n(ks[13], (DEPTH, D_MODEL, D_FF), D_MODEL ** -0.5),
        'w_mlp2': n(ks[14], (DEPTH, D_FF, D_MODEL), D_FF ** -0.5),
        'final_norm_g': 1.0 + n(ks[15], (D_MODEL,), 0.1),
    }


def reference(x, c, ctx, c_ctx, w_mod, b_mod, w_in, q_norm_g, w_uq, kv_norm_g, w_ukv,
              conv_w, w_out, w_mlp1, w_mlp2, final_norm_g):
    rows = x.shape[1] // GRID_W
    cos, sin = rope_tables(rows)
    ctx_s = ctx
    for i in range(DEPTH):
        sh1, sc1, g1, sh2, sc2, g2 = [m[:, None, :] for m in adaln(c, w_mod[i], b_mod[i])]
        sh1c, sc1c, g1c, sh2c, sc2c, g2c = adaln(c_ctx, w_mod[i], b_mod[i])

        z = modulate(x, sh1, sc1) @ w_in[i]
        zc = modulate(ctx_s, sh1c, sc1c) @ w_in[i]

        q = mla_q(z, q_norm_g[i], w_uq[i], cos, sin)
        k, v = mla_kv(z, kv_norm_g[i], w_ukv[i], cos, sin)
        kc, vc = mla_kv(zc, kv_norm_g[i], w_ukv[i], None, None)
        k_all = jnp.concatenate([k, kc], axis=1)
        v_all = jnp.concatenate([v, vc], axis=1)
        attn = attention_blocked(q, k_all, v_all)
        conv = short_conv(z, conv_w[i])
        x = x + g1 * (jnp.concatenate([attn, conv], axis=-1) @ w_out[i])

        x = x + g2 * sq_relu_mlp(modulate(x, sh2, sc2), w_mlp1[i], w_mlp2[i])

        if i + 1 < DEPTH:
            qc = mla_q(zc, q_norm_g[i], w_uq[i], None, None)
            attn_c = attention_dense(qc, kc, vc)
            conv_c = short_conv(zc, conv_w[i])
            ctx_s = ctx_s + g1c * (jnp.concatenate([attn_c, conv_c], axis=-1) @ w_out[i])
            ctx_s = ctx_s + g2c * sq_relu_mlp(modulate(ctx_s, sh2c, sc2c), w_mlp1[i], w_mlp2[i])

    return rmsnorm(x) * final_norm_g
```

```python
import functools
import math

import jax
import jax.numpy as jnp
import numpy as np
from jax import lax
from jax.experimental import pallas as pl
from jax.experimental.pallas import tpu as pltpu

GRID_W = 64
N_HEADS = 8
QK_NOPE_DIM = 64
QK_ROPE_DIM = 32
V_HEAD_DIM = 64
Q_LORA_RANK = 256
KV_LORA_RANK = 128
CONV_K = 3
ROPE_THETA = 10000.0
EPS = 1e-6
QK_DIM = QK_NOPE_DIM + QK_ROPE_DIM
ROPE_HALF = QK_ROPE_DIM // 2
MLA_IN = Q_LORA_RANK + KV_LORA_RANK + QK_ROPE_DIM
ATTN_SCALE = 1.0 / math.sqrt(QK_DIM)
LOG2E = math.log2(math.e)

LANES = 128
SUBLANES_F32 = 8
VMEM_BYTES_V7X = 64 * 1024 * 1024

HEAD_SLOT = LANES
HEADS_PER_STEP = 2
HALO = SUBLANES_F32

F32 = jnp.float32
BF16 = jnp.bfloat16


class _Plan:
    tok_tile = 512
    q_tile = 512
    kv_chunk = 512
    ff_chunk = 1024
    mod_cols = 1536
    vmem_limit = 56 * 1024 * 1024
    assert vmem_limit < VMEM_BYTES_V7X


def _rms(x):
    return x * lax.rsqrt(jnp.mean(x * x, axis=-1, keepdims=True) + EPS)


def _dot(a, b):
    return jnp.dot(a, b, preferred_element_type=F32)


def _dot_nt(a, b):
    return lax.dot_general(a, b, (((1,), (1,)), ((), ())), preferred_element_type=F32)


def _rope(t, cos_tab, sin_tab):
    return t * cos_tab + pltpu.roll(t, LANES - ROPE_HALF, 1) * sin_tab


def _adaln_kernel(c_ref, w_ref, b_ref, o_ref):
    c = c_ref[...]
    a = (c / (1.0 + jnp.exp(-c))).astype(BF16)
    o_ref[...] = _dot(a, w_ref[...].astype(BF16)) + b_ref[...]


def _adaln(cvec, w_mod, b_mod):
    rows, d = cvec.shape
    n = w_mod.shape[1]
    tn = _Plan.mod_cols
    return pl.pallas_call(
        _adaln_kernel,
        grid=(n // tn,),
        in_specs=[pl.BlockSpec((rows, d), lambda j: (0, 0)),
                  pl.BlockSpec((d, tn), lambda j: (0, j)),
                  pl.BlockSpec((1, tn), lambda j: (0, j))],
        out_specs=pl.BlockSpec((rows, tn), lambda j: (0, j)),
        out_shape=jax.ShapeDtypeStruct((rows, n), F32),
        compiler_params=pltpu.CompilerParams(dimension_semantics=("arbitrary",),
                                             vmem_limit_bytes=_Plan.vmem_limit),
        name="adaln",
    )(cvec, w_mod, b_mod)


def _kv_from_z(z_ckv, z_rope, kvg_ref, wukv_ref, ck_ref, sk_ref, k_ref, v_ref):
    ckv = _rms(z_ckv) * kvg_ref[...]
    kv = _dot(ckv.astype(BF16), wukv_ref[...])
    k_rope = _rope(z_rope, ck_ref[...], sk_ref[...])
    for h in range(N_HEADS):
        sl = slice(h * HEAD_SLOT, (h + 1) * HEAD_SLOT)
        k_ref[0, :, sl] = (kv[:, sl] + k_rope).astype(BF16)
    v_ref[0] = kv[:, N_HEADS * HEAD_SLOT:].astype(BF16)


def _proj_kernel(xp_ref, x_ref, xn_ref, mod_ref, win_ref, qg_ref, wuq_ref, kvg_ref, wukv_ref, cw_ref,
                 cq_ref, sq_ref, ck_ref, sk_ref, q_ref, k_ref, v_ref, y_ref, u_scr):
    i = pl.program_id(1)
    n = pl.num_programs(1)
    tm = x_ref.shape[1]
    cw = y_ref.shape[2]
    shift, scale = mod_ref[0, 0:1, :], mod_ref[0, 1:2, :]
    xe = jnp.concatenate([xp_ref[0], x_ref[0], xn_ref[0]], axis=0)
    hmod = (_rms(xe) * (1.0 + scale) + shift).astype(BF16)
    z = _dot(hmod, win_ref[...])
    zm = z[HALO:HALO + tm]

    cq = _rms(zm[:, :Q_LORA_RANK]) * qg_ref[...]
    qf = _dot(cq.astype(BF16), wuq_ref[...])
    cos_q, sin_q = cq_ref[...], sq_ref[...]
    for h in range(N_HEADS):
        sl = slice(h * HEAD_SLOT, (h + 1) * HEAD_SLOT)
        q_ref[0, :, sl] = _rope(qf[:, sl], cos_q, sin_q).astype(BF16)

    c0 = Q_LORA_RANK
    c1 = c0 + KV_LORA_RANK
    c2 = c1 + LANES
    _kv_from_z(zm[:, c0:c1], zm[:, c1:c2], kvg_ref, wukv_ref, ck_ref, sk_ref, k_ref, v_ref)

    gate_b = zm[:, c2:c2 + cw]
    u = z[:, c2 + cw:c2 + 2 * cw] * z[:, c2 + 2 * cw:c2 + 3 * cw]
    row = lax.broadcasted_iota(jnp.int32, (tm + 2 * HALO, 1), 0)
    keep = ((row >= HALO) | (i > 0)) & ((row < HALO + tm) | (i < n - 1))
    u_scr[...] = jnp.where(keep, u, 0.0)
    w = cw_ref[...]
    y = (w[0:1] * u_scr[HALO - 1:HALO - 1 + tm]
         + w[1:2] * u_scr[HALO:HALO + tm]
         + w[2:3] * u_scr[HALO + 1:HALO + 1 + tm])
    y_ref[0] = (gate_b * y).astype(BF16)


def _ctx_kv_kernel(x_ref, mod_ref, win_ref, kvg_ref, wukv_ref, ck_ref, sk_ref, k_ref, v_ref):
    shift, scale = mod_ref[0, 0:1, :], mod_ref[0, 1:2, :]
    hmod = (_rms(x_ref[0]) * (1.0 + scale) + shift).astype(BF16)
    z = _dot(hmod, win_ref[...])
    _kv_from_z(z[:, :KV_LORA_RANK], z[:, KV_LORA_RANK:], kvg_ref, wukv_ref, ck_ref, sk_ref, k_ref, v_ref)


def _const_spec(shape):
    return pl.BlockSpec(shape, lambda *_: (0,) * len(shape))


def _project_tokens(x, mod, w_in_p, q_g, w_uq_p, kv_g, w_ukv_p, conv_w, tabs):
    b, s, d = x.shape
    tm = _Plan.tok_tile
    cw = conv_w.shape[1]
    nh = N_HEADS * HEAD_SLOT
    nv = N_HEADS * V_HEAD_DIM
    halo_blocks = tm // HALO
    last_halo = s // HALO - 1
    tab_spec = pl.BlockSpec((tm, LANES), lambda bi, i: (i, 0))
    tok = lambda w: pl.BlockSpec((1, tm, w), lambda bi, i: (bi, i, 0))
    return pl.pallas_call(
        _proj_kernel,
        grid=(b, s // tm),
        in_specs=[
            pl.BlockSpec((1, HALO, d), lambda bi, i: (bi, jnp.maximum(i * halo_blocks - 1, 0), 0)),
            tok(d),
            pl.BlockSpec((1, HALO, d), lambda bi, i: (bi, jnp.minimum((i + 1) * halo_blocks, last_halo), 0)),
            pl.BlockSpec((1,) + mod.shape[1:], lambda bi, i: (bi, 0, 0)),
            _const_spec(w_in_p.shape), _const_spec(q_g.shape), _const_spec(w_uq_p.shape),
            _const_spec(kv_g.shape), _const_spec(w_ukv_p.shape), _const_spec(conv_w.shape),
            tab_spec, tab_spec, tab_spec, tab_spec,
        ],
        out_specs=[tok(nh), tok(nh), tok(nv), tok(cw)],
        out_shape=[jax.ShapeDtypeStruct((b, s, nh), BF16), jax.ShapeDtypeStruct((b, s, nh), BF16),
                   jax.ShapeDtypeStruct((b, s, nv), BF16), jax.ShapeDtypeStruct((b, s, cw), BF16)],
        scratch_shapes=[pltpu.VMEM((tm + 2 * HALO, cw), F32)],
        compiler_params=pltpu.CompilerParams(dimension_semantics=("arbitrary", "arbitrary"),
                                             vmem_limit_bytes=_Plan.vmem_limit),
        name="token_proj",
    )(x, x, x, mod, w_in_p, q_g, w_uq_p, kv_g, w_ukv_p, conv_w, *tabs)


def _project_ctx(ctx, mod, ctx_row, w_in_p, kv_g, w_ukv_p, tabs):
    b, l, d = ctx.shape
    nh = N_HEADS * HEAD_SLOT
    nv = N_HEADS * V_HEAD_DIM
    kv_cols = 2 * LANES
    tok = lambda w: pl.BlockSpec((1, l, w), lambda bi: (bi, 0, 0))
    return pl.pallas_call(
        _ctx_kv_kernel,
        grid=(b,),
        in_specs=[
            tok(d),
            pl.BlockSpec((1,) + mod.shape[1:], lambda bi: (ctx_row, 0, 0)),
            pl.BlockSpec((d, kv_cols), lambda bi: (0, Q_LORA_RANK // kv_cols)),
            _const_spec(kv_g.shape), _const_spec(w_ukv_p.shape),
            _const_spec(tabs[0].shape), _const_spec(tabs[1].shape),
        ],
        out_specs=[tok(nh), tok(nv)],
        out_shape=[jax.ShapeDtypeStruct((b, l, nh), BF16), jax.ShapeDtypeStruct((b, l, nv), BF16)],
        compiler_params=pltpu.CompilerParams(dimension_semantics=("arbitrary",),
                                             vmem_limit_bytes=_Plan.vmem_limit),
        name="ctx_proj",
    )(ctx, mod, w_in_p, kv_g, w_ukv_p, *tabs)


def _lane_tile_reduce(op, a):
    return functools.reduce(op, [a[:, t * LANES:(t + 1) * LANES] for t in range(a.shape[1] // LANES)])


def _attn_kernel(q_ref, k_ref, v_ref, kc_ref, vc_ref, o_ref, s_scr):
    s_len = k_ref.shape[1]
    c_len = kc_ref.shape[1]
    tk = _Plan.kv_chunk
    chunks = [(k_ref, v_ref, c * tk, tk, c * tk) for c in range(s_len // tk)]
    chunks.append((kc_ref, vc_ref, 0, c_len, s_len))
    outs = []
    for j in range(HEADS_PER_STEP):
        hs = slice(j * HEAD_SLOT, (j + 1) * HEAD_SLOT)
        q = q_ref[0, :, hs]
        m_run = None
        for kr, _, r0, rows, c0 in chunks:
            s = _dot_nt(q, kr[0, r0:r0 + rows, hs])
            s_scr[:, c0:c0 + rows] = s
            cm = _lane_tile_reduce(jnp.maximum, s)
            m_run = cm if m_run is None else jnp.maximum(m_run, cm)
        m = jnp.max(m_run, axis=-1, keepdims=True)
        l_run = None
        acc = None
        for _, vr, r0, rows, c0 in chunks:
            p = jnp.exp2(s_scr[:, c0:c0 + rows] - m)
            cl = _lane_tile_reduce(jnp.add, p)
            l_run = cl if l_run is None else l_run + cl
            pv = _dot(p.astype(BF16), vr[0, r0:r0 + rows, :])
            acc = pv if acc is None else acc + pv
        outs.append(acc / jnp.sum(l_run, axis=-1, keepdims=True))
    lane = lax.broadcasted_iota(jnp.int32, outs[0].shape, 1)
    o_ref[0] = jnp.where(lane < V_HEAD_DIM, outs[0], outs[1]).astype(BF16)


def _attention(q, k, v, kc, vc):
    b, s, _ = q.shape
    l = kc.shape[1]
    tq = _Plan.q_tile
    groups = N_HEADS // HEADS_PER_STEP
    qk_w = HEADS_PER_STEP * HEAD_SLOT
    v_w = HEADS_PER_STEP * V_HEAD_DIM
    return pl.pallas_call(
        _attn_kernel,
        grid=(b, groups, s // tq),
        in_specs=[
            pl.BlockSpec((1, tq, qk_w), lambda bi, g, i: (bi, i, g)),
            pl.BlockSpec((1, s, qk_w), lambda bi, g, i: (bi, 0, g)),
            pl.BlockSpec((1, s, v_w), lambda bi, g, i: (bi, 0, g)),
            pl.BlockSpec((1, l, qk_w), lambda bi, g, i: (bi, 0, g)),
            pl.BlockSpec((1, l, v_w), lambda bi, g, i: (bi, 0, g)),
        ],
        out_specs=pl.BlockSpec((1, tq, v_w), lambda bi, g, i: (bi, i, g)),
        out_shape=jax.ShapeDtypeStruct((b, s, N_HEADS * V_HEAD_DIM), BF16),
        scratch_shapes=[pltpu.VMEM((tq, s + l), F32)],
        compiler_params=pltpu.CompilerParams(dimension_semantics=("arbitrary",) * 3,
                                             vmem_limit_bytes=_Plan.vmem_limit),
        name="attention",
    )(q, k, v, kc, vc)


def _mlp_kernel(x_ref, a_ref, y_ref, mod_ref, woa_ref, woc_ref, w1_ref, w2_ref, gf_ref, o_ref):
    gate1 = mod_ref[0, 2:3, :]
    shift2, scale2, gate2 = mod_ref[0, 3:4, :], mod_ref[0, 4:5, :], mod_ref[0, 5:6, :]
    mix = _dot(a_ref[0], woa_ref[...]) + _dot(y_ref[0], woc_ref[...])
    x1 = x_ref[0] + gate1 * mix
    hmod = (_rms(x1) * (1.0 + scale2) + shift2).astype(BF16)
    fc = _Plan.ff_chunk
    acc = None
    for j in range(w1_ref.shape[1] // fc):
        u = jnp.maximum(_dot(hmod, w1_ref[:, j * fc:(j + 1) * fc]), 0.0)
        part = _dot((u * u).astype(BF16), w2_ref[j * fc:(j + 1) * fc, :])
        acc = part if acc is None else acc + part
    x2 = x1 + gate2 * acc
    o_ref[0] = _rms(x2) * gf_ref[...]


def _out_proj_mlp(x, attn, conv, mod, wo_a, wo_c, w1, w2, gf):
    b, s, d = x.shape
    tm = _Plan.tok_tile
    tok = lambda w: pl.BlockSpec((1, tm, w), lambda bi, i: (bi, i, 0))
    return pl.pallas_call(
        _mlp_kernel,
        grid=(b, s // tm),
        in_specs=[tok(d), tok(attn.shape[2]), tok(conv.shape[2]),
                  pl.BlockSpec((1,) + mod.shape[1:], lambda bi, i: (bi, 0, 0)),
                  _const_spec(wo_a.shape), _const_spec(wo_c.shape),
                  _const_spec(w1.shape), _const_spec(w2.shape), _const_spec(gf.shape)],
        out_specs=tok(d),
        out_shape=jax.ShapeDtypeStruct((b, s, d), F32),
        compiler_params=pltpu.CompilerParams(dimension_semantics=("arbitrary", "arbitrary"),
                                             vmem_limit_bytes=_Plan.vmem_limit),
        name="out_proj_mlp",
    )(x, attn, conv, mod, wo_a, wo_c, w1, w2, gf)


def _rope_lane_order():
    half = QK_ROPE_DIM // 4
    x1 = np.concatenate([np.arange(half), 2 * half + np.arange(half)])
    x2 = x1 + half
    return np.concatenate([x1, x2, x1, x2])


def _rope_tables(rows, scale):
    half = QK_ROPE_DIM // 4
    pos = jnp.arange(rows * GRID_W)
    freqs = ROPE_THETA ** (-jnp.arange(0, 2 * half, 2, dtype=F32) / (2 * half))
    ang = jnp.concatenate([(pos // GRID_W).astype(F32)[:, None] * freqs,
                           (pos % GRID_W).astype(F32)[:, None] * freqs], axis=1)
    cos, sin = jnp.cos(ang), jnp.sin(ang)
    zeros = lambda w: jnp.zeros((pos.shape[0], w), F32)
    pad = LANES - QK_NOPE_DIM - 2 * ROPE_HALF
    cos_k = jnp.concatenate([zeros(QK_NOPE_DIM), cos, cos, zeros(pad)], axis=1)
    sin_k = jnp.concatenate([zeros(QK_NOPE_DIM), -sin, sin, zeros(pad)], axis=1)
    cos_q = jnp.concatenate([jnp.ones((pos.shape[0], QK_NOPE_DIM), F32), cos, cos, zeros(pad)], axis=1)
    return cos_q * scale, sin_k * scale, cos_k, sin_k


def kernel(x, c, ctx, c_ctx, w_mod, b_mod, w_in, q_norm_g, w_uq, kv_norm_g, w_ukv,
           conv_w, w_out, w_mlp1, w_mlp2, final_norm_g):
    b, s, d = x.shape
    l = ctx.shape[1]
    depth = w_mod.shape[0]
    assert depth == 1, "single-layer configuration"
    assert s % _Plan.tok_tile == 0 and s % _Plan.q_tile == 0 and s % _Plan.kv_chunk == 0
    assert s % GRID_W == 0 and l % LANES == 0 and N_HEADS % HEADS_PER_STEP == 0
    cw = conv_w.shape[2]
    assert w_in.shape[2] == MLA_IN + 3 * cw and d == N_HEADS * V_HEAD_DIM + cw

    ctx_row = b
    rows = -(-(b + 1) // SUBLANES_F32) * SUBLANES_F32
    cvec = jnp.concatenate([c, c_ctx[None, :], jnp.zeros((rows - b - 1, d), F32)], axis=0)
    mod = _adaln(cvec, w_mod[0], b_mod[0][None, :]).reshape(rows, 6, d)

    order = _rope_lane_order()
    pad64 = jnp.zeros((d, LANES - len(order)), F32)
    wi = w_in[0]
    w_in_p = jnp.concatenate(
        [wi[:, :Q_LORA_RANK + KV_LORA_RANK], pad64, wi[:, Q_LORA_RANK + KV_LORA_RANK + order], wi[:, MLA_IN:]],
        axis=1).astype(BF16)
    wq = w_uq[0].reshape(Q_LORA_RANK, N_HEADS, QK_DIM)
    w_uq_p = jnp.concatenate([wq[:, :, :QK_NOPE_DIM], wq[:, :, QK_NOPE_DIM + order]], axis=2)
    w_uq_p = w_uq_p.reshape(Q_LORA_RANK, N_HEADS * HEAD_SLOT).astype(BF16)
    wkv = w_ukv[0].reshape(KV_LORA_RANK, N_HEADS, QK_NOPE_DIM + V_HEAD_DIM)
    wk = jnp.concatenate([wkv[:, :, :QK_NOPE_DIM],
                          jnp.zeros((KV_LORA_RANK, N_HEADS, HEAD_SLOT - QK_NOPE_DIM), F32)], axis=2)
    w_ukv_p = jnp.concatenate([wk.reshape(KV_LORA_RANK, N_HEADS * HEAD_SLOT),
                               wkv[:, :, QK_NOPE_DIM:].reshape(KV_LORA_RANK, N_HEADS * V_HEAD_DIM)],
                              axis=1).astype(BF16)

    cos_q, sin_q, cos_k, sin_k = _rope_tables(s // GRID_W, ATTN_SCALE * LOG2E)
    lane = jnp.arange(LANES)
    ctx_cos = jnp.broadcast_to(((lane >= QK_NOPE_DIM) & (lane < QK_DIM)).astype(F32), (l, LANES))
    ctx_sin = jnp.zeros((l, LANES), F32)

    q, k, v, conv = _project_tokens(x, mod, w_in_p, q_norm_g, w_uq_p, kv_norm_g, w_ukv_p, conv_w[0],
                                    (cos_q, sin_q, cos_k, sin_k))
    kc, vc = _project_ctx(ctx, mod, ctx_row, w_in_p, kv_norm_g, w_ukv_p, (ctx_cos, ctx_sin))
    attn = _attention(q, k, v, kc, vc)

    nv = N_HEADS * V_HEAD_DIM
    wo = w_out[0].astype(BF16)
    return _out_proj_mlp(x, attn, conv, mod, wo[:nv], wo[nv:], w_mlp1[0].astype(BF16),
                         w_mlp2[0].astype(BF16), final_norm_g[None, :])
```

```python
import functools
import math

import jax
import jax.numpy as jnp
import numpy as np
from jax import lax
from jax.experimental import pallas as pl
from jax.experimental.pallas import tpu as pltpu

GRID_W = 64
N_HEADS = 8
QK_NOPE_DIM = 64
QK_ROPE_DIM = 32
V_HEAD_DIM = 64
Q_LORA_RANK = 256
KV_LORA_RANK = 128
CONV_K = 3
ROPE_THETA = 10000.0
EPS = 1e-6
QK_DIM = QK_NOPE_DIM + QK_ROPE_DIM
ROPE_HALF = QK_ROPE_DIM // 2
MLA_IN = Q_LORA_RANK + KV_LORA_RANK + QK_ROPE_DIM
ATTN_SCALE = 1.0 / math.sqrt(QK_DIM)
LOG2E = math.log2(math.e)

LANES = 128
SUBLANES_F32 = 8
VMEM_BYTES_V7X = 64 * 1024 * 1024

HEAD_SLOT = LANES
HEADS_PER_STEP = 2
HALO = SUBLANES_F32

F32 = jnp.float32
BF16 = jnp.bfloat16


class _Plan:
    tok_tile = 512
    q_tile = 512
    kv_chunk = 512
    ff_chunk = 1024
    mod_cols = 1536
    vmem_limit = 56 * 1024 * 1024
    assert vmem_limit < VMEM_BYTES_V7X


def _rms(x):
    return x * lax.rsqrt(jnp.mean(x * x, axis=-1, keepdims=True) + EPS)


def _dot(a, b):
    return jnp.dot(a, b, preferred_element_type=F32)


def _dot_nt(a, b):
    return lax.dot_general(a, b, (((1,), (1,)), ((), ())), preferred_element_type=F32)


def _rope(t, cos_tab, sin_tab):
    return t * cos_tab + pltpu.roll(t, LANES - ROPE_HALF, 1) * sin_tab


def _adaln_kernel(c_ref, w_ref, b_ref, o_ref):
    c = c_ref[...]
    a = (c / (1.0 + jnp.exp(-c))).astype(BF16)
    o_ref[...] = _dot(a, w_ref[...].astype(BF16)) + b_ref[...]


def _adaln(cvec, w_mod, b_mod):
    rows, d = cvec.shape
    n = w_mod.shape[1]
    tn = _Plan.mod_cols
    return pl.pallas_call(
        _adaln_kernel,
        grid=(n // tn,),
        in_specs=[pl.BlockSpec((rows, d), lambda j: (0, 0)),
                  pl.BlockSpec((d, tn), lambda j: (0, j)),
                  pl.BlockSpec((1, tn), lambda j: (0, j))],
        out_specs=pl.BlockSpec((rows, tn), lambda j: (0, j)),
        out_shape=jax.ShapeDtypeStruct((rows, n), F32),
        compiler_params=pltpu.CompilerParams(dimension_semantics=("arbitrary",),
                                             vmem_limit_bytes=_Plan.vmem_limit),
        name="adaln",
    )(cvec, w_mod, b_mod)


def _kv_from_z(z_ckv, z_rope, kvg_ref, wukv_ref, ck_ref, sk_ref, k_ref, v_ref):
    ckv = _rms(z_ckv) * kvg_ref[...]
    kv = _dot(ckv.astype(BF16), wukv_ref[...])
    k_rope = _rope(z_rope, ck_ref[...], sk_ref[...])
    for h in range(N_HEADS):
        sl = slice(h * HEAD_SLOT, (h + 1) * HEAD_SLOT)
        k_ref[0, :, sl] = (kv[:, sl] + k_rope).astype(BF16)
    v_ref[0] = kv[:, N_HEADS * HEAD_SLOT:].astype(BF16)


def _proj_kernel(xp_ref, x_ref, xn_ref, mod_ref, win_ref, qg_ref, wuq_ref, kvg_ref, wukv_ref, cw_ref,
                 cq_ref, sq_ref, ck_ref, sk_ref, q_ref, k_ref, v_ref, y_ref, u_scr):
    i = pl.program_id(1)
    n = pl.num_programs(1)
    tm = x_ref.shape[1]
    cw = y_ref.shape[2]
    shift, scale = mod_ref[0, 0:1, :], mod_ref[0, 1:2, :]
    xe = jnp.concatenate([xp_ref[0], x_ref[0], xn_ref[0]], axis=0)
    hmod = (_rms(xe) * (1.0 + scale) + shift).astype(BF16)
    z = _dot(hmod, win_ref[...])
    zm = z[HALO:HALO + tm]

    cq = _rms(zm[:, :Q_LORA_RANK]) * qg_ref[...]
    qf = _dot(cq.astype(BF16), wuq_ref[...])
    cos_q, sin_q = cq_ref[...], sq_ref[...]
    for h in range(N_HEADS):
        sl = slice(h * HEAD_SLOT, (h + 1) * HEAD_SLOT)
        q_ref[0, :, sl] = _rope(qf[:, sl], cos_q, sin_q).astype(BF16)

    c0 = Q_LORA_RANK
    c1 = c0 + KV_LORA_RANK
    c2 = c1 + LANES
    _kv_from_z(zm[:, c0:c1], zm[:, c1:c2], kvg_ref, wukv_ref, ck_ref, sk_ref, k_ref, v_ref)

    gate_b = zm[:, c2:c2 + cw]
    u = z[:, c2 + cw:c2 + 2 * cw] * z[:, c2 + 2 * cw:c2 + 3 * cw]
    row = lax.broadcasted_iota(jnp.int32, (tm + 2 * HALO, 1), 0)
    keep = ((row >= HALO) | (i > 0)) & ((row < HALO + tm) | (i < n - 1))
    u_scr[...] = jnp.where(keep, u, 0.0)
    w = cw_ref[...]
    y = (w[0:1] * u_scr[HALO - 1:HALO - 1 + tm]
         + w[1:2] * u_scr[HALO:HALO + tm]
         + w[2:3] * u_scr[HALO + 1:HALO + 1 + tm])
    y_ref[0] = (gate_b * y).astype(BF16)


def _ctx_kv_kernel(x_ref, mod_ref, win_ref, kvg_ref, wukv_ref, ck_ref, sk_ref, k_ref, v_ref):
    shift, scale = mod_ref[0, 0:1, :], mod_ref[0, 1:2, :]
    hmod = (_rms(x_ref[0]) * (1.0 + scale) + shift).astype(BF16)
    z = _dot(hmod, win_ref[...])
    _kv_from_z(z[:, :KV_LORA_RANK], z[:, KV_LORA_RANK:], kvg_ref, wukv_ref, ck_ref, sk_ref, k_ref, v_ref)


def _const_spec(shape):
    return pl.BlockSpec(shape, lambda *_: (0,) * len(shape))


def _project_tokens(x, mod, w_in_p, q_g, w_uq_p, kv_g, w_ukv_p, conv_w, tabs):
    b, s, d = x.shape
    tm = _Plan.tok_tile
    cw = conv_w.shape[1]
    nh = N_HEADS * HEAD_SLOT
    nv = N_HEADS * V_HEAD_DIM
    halo_blocks = tm // HALO
    last_halo = s // HALO - 1
    tab_spec = pl.BlockSpec((tm, LANES), lambda bi, i: (i, 0))
    tok = lambda w: pl.BlockSpec((1, tm, w), lambda bi, i: (bi, i, 0))
    return pl.pallas_call(
        _proj_kernel,
        grid=(b, s // tm),
        in_specs=[
            pl.BlockSpec((1, HALO, d), lambda bi, i: (bi, jnp.maximum(i * halo_blocks - 1, 0), 0)),
            tok(d),
            pl.BlockSpec((1, HALO, d), lambda bi, i: (bi, jnp.minimum((i + 1) * halo_blocks, last_halo), 0)),
            pl.BlockSpec((1,) + mod.shape[1:], lambda bi, i: (bi, 0, 0)),
            _const_spec(w_in_p.shape), _const_spec(q_g.shape), _const_spec(w_uq_p.shape),
            _const_spec(kv_g.shape), _const_spec(w_ukv_p.shape), _const_spec(conv_w.shape),
            tab_spec, tab_spec, tab_spec, tab_spec,
        ],
        out_specs=[tok(nh), tok(nh), tok(nv), tok(cw)],
        out_shape=[jax.ShapeDtypeStruct((b, s, nh), BF16), jax.ShapeDtypeStruct((b, s, nh), BF16),
                   jax.ShapeDtypeStruct((b, s, nv), BF16), jax.ShapeDtypeStruct((b, s, cw), BF16)],
        scratch_shapes=[pltpu.VMEM((tm + 2 * HALO, cw), F32)],
        compiler_params=pltpu.CompilerParams(dimension_semantics=("arbitrary", "arbitrary"),
                                             vmem_limit_bytes=_Plan.vmem_limit),
        name="token_proj",
    )(x, x, x, mod, w_in_p, q_g, w_uq_p, kv_g, w_ukv_p, conv_w, *tabs)


def _project_ctx(ctx, mod, ctx_row, w_in_p, kv_g, w_ukv_p, tabs):
    b, l, d = ctx.shape
    nh = N_HEADS * HEAD_SLOT
    nv = N_HEADS * V_HEAD_DIM
    kv_cols = 2 * LANES
    tok = lambda w: pl.BlockSpec((1, l, w), lambda bi: (bi, 0, 0))
    return pl.pallas_call(
        _ctx_kv_kernel,
        grid=(b,),
        in_specs=[
            tok(d),
            pl.BlockSpec((1,) + mod.shape[1:], lambda bi: (ctx_row, 0, 0)),
            pl.BlockSpec((d, kv_cols), lambda bi: (0, Q_LORA_RANK // kv_cols)),
            _const_spec(kv_g.shape), _const_spec(w_ukv_p.shape),
            _const_spec(tabs[0].shape), _const_spec(tabs[1].shape),
        ],
        out_specs=[tok(nh), tok(nv)],
        out_shape=[jax.ShapeDtypeStruct((b, l, nh), BF16), jax.ShapeDtypeStruct((b, l, nv), BF16)],
        compiler_params=pltpu.CompilerParams(dimension_semantics=("arbitrary",),
                                             vmem_limit_bytes=_Plan.vmem_limit),
        name="ctx_proj",
    )(ctx, mod, w_in_p, kv_g, w_ukv_p, *tabs)


def _lane_tile_reduce(op, a):
    return functools.reduce(op, [a[:, t * LANES:(t + 1) * LANES] for t in range(a.shape[1] // LANES)])


def _key_chunks(main_ref, ctx_ref):
    s_len, tk = main_ref.shape[1], _Plan.kv_chunk
    return [(main_ref, c * tk, tk, c * tk) for c in range(s_len // tk)] + [(ctx_ref, 0, ctx_ref.shape[1], s_len)]


def _scores(q_ref, k_ref, kc_ref, head, s_scr):
    hs = slice(head * HEAD_SLOT, (head + 1) * HEAD_SLOT)
    q = q_ref[0, :, hs]
    m_run = None
    for ref, r0, rows, c0 in _key_chunks(k_ref, kc_ref):
        s = _dot_nt(q, ref[0, r0:r0 + rows, hs])
        s_scr[:, c0:c0 + rows] = s
        cm = _lane_tile_reduce(jnp.maximum, s)
        m_run = cm if m_run is None else jnp.maximum(m_run, cm)
    return jnp.broadcast_to(jnp.max(m_run, axis=-1, keepdims=True), m_run.shape)


def _weighted_values(s_scr, m_rep, v_ref, vc_ref):
    acc = None
    for ref, r0, rows, c0 in _key_chunks(v_ref, vc_ref):
        tiles = [jnp.exp2(s_scr[:, c0 + t * LANES:c0 + (t + 1) * LANES] - m_rep) for t in range(rows // LANES)]
        rhs = jnp.concatenate([ref[0, r0:r0 + rows, :], jnp.ones((rows, LANES), BF16)], axis=1)
        pv = _dot(jnp.concatenate(tiles, axis=1).astype(BF16), rhs)
        acc = pv if acc is None else acc + pv
    return acc


def _attn_kernel(q_ref, k_ref, kc_ref, vp_ref, vcp_ref, vn_ref, vcn_ref, o_ref,
                 s0_scr, s1_scr, m0_scr, m1_scr, o0_scr):
    t = pl.program_id(0)

    @pl.when(t == 0)
    def _():
        s1_scr[...] = jnp.zeros_like(s1_scr)
        m1_scr[...] = jnp.zeros_like(m1_scr)
        o0_scr[...] = jnp.ones_like(o0_scr)

    @pl.when(t >= 0)
    def _():
        m0_scr[...] = _scores(q_ref, k_ref, kc_ref, 0, s0_scr)
        acc1 = _weighted_values(s1_scr, m1_scr[...], vp_ref, vcp_ref)
        acc0 = o0_scr[...]
        o0 = acc0[:, :LANES] / acc0[:, LANES:]
        o1 = acc1[:, :LANES] / acc1[:, LANES:]
        lane = lax.broadcasted_iota(jnp.int32, o0.shape, 1)
        o_ref[0] = jnp.where(lane < V_HEAD_DIM, o0, o1).astype(BF16)

    @pl.when(t < pl.num_programs(0))
    def _():
        m1_scr[...] = _scores(q_ref, k_ref, kc_ref, 1, s1_scr)
        o0_scr[...] = _weighted_values(s0_scr, m0_scr[...], vn_ref, vcn_ref)


def _attention(q, k, v, kc, vc):
    b, s, _ = q.shape
    l = kc.shape[1]
    tq = _Plan.q_tile
    groups = N_HEADS // HEADS_PER_STEP
    n_q = s // tq
    tiles = b * groups * n_q
    pair_w = HEADS_PER_STEP * HEAD_SLOT
    o_w = HEADS_PER_STEP * V_HEAD_DIM

    def decode(u):
        return u // (groups * n_q), (u // n_q) % groups, u % n_q

    def cur(t):
        return decode(jnp.minimum(t, tiles - 1))

    def prev(t):
        return decode(jnp.maximum(t - 1, 0))

    def keys_of(which, rows, width):
        return pl.BlockSpec((1, rows, width), lambda t: (which(t)[0], 0, which(t)[1]))

    def rows_of(which, width):
        return pl.BlockSpec((1, tq, width), lambda t: (which(t)[0], which(t)[2], which(t)[1]))

    return pl.pallas_call(
        _attn_kernel,
        grid=(tiles + 1,),
        in_specs=[rows_of(cur, pair_w), keys_of(cur, s, pair_w), keys_of(cur, l, pair_w),
                  keys_of(prev, s, o_w), keys_of(prev, l, o_w), keys_of(cur, s, o_w), keys_of(cur, l, o_w)],
        out_specs=rows_of(prev, o_w),
        out_shape=jax.ShapeDtypeStruct((b, s, N_HEADS * V_HEAD_DIM), BF16),
        scratch_shapes=[pltpu.VMEM((tq, s + l), F32), pltpu.VMEM((tq, s + l), F32),
                        pltpu.VMEM((tq, LANES), F32), pltpu.VMEM((tq, LANES), F32),
                        pltpu.VMEM((tq, 2 * LANES), F32)],
        compiler_params=pltpu.CompilerParams(dimension_semantics=("arbitrary",),
                                             vmem_limit_bytes=_Plan.vmem_limit),
        name="attention",
    )(q, k, kc, v, vc, v, vc)


def _mlp_kernel(x_ref, a_ref, y_ref, mod_ref, woa_ref, woc_ref, w1_ref, w2_ref, gf_ref, o_ref):
    gate1 = mod_ref[0, 2:3, :]
    shift2, scale2, gate2 = mod_ref[0, 3:4, :], mod_ref[0, 4:5, :], mod_ref[0, 5:6, :]
    mix = _dot(a_ref[0], woa_ref[...]) + _dot(y_ref[0], woc_ref[...])
    x1 = x_ref[0] + gate1 * mix
    hmod = (_rms(x1) * (1.0 + scale2) + shift2).astype(BF16)
    fc = _Plan.ff_chunk
    acc = None
    for j in range(w1_ref.shape[1] // fc):
        u = jnp.maximum(_dot(hmod, w1_ref[:, j * fc:(j + 1) * fc]), 0.0)
        part = _dot((u * u).astype(BF16), w2_ref[j * fc:(j + 1) * fc, :])
        acc = part if acc is None else acc + part
    x2 = x1 + gate2 * acc
    o_ref[0] = _rms(x2) * gf_ref[...]


def _out_proj_mlp(x, attn, conv, mod, wo_a, wo_c, w1, w2, gf):
    b, s, d = x.shape
    tm = _Plan.tok_tile
    tok = lambda w: pl.BlockSpec((1, tm, w), lambda bi, i: (bi, i, 0))
    return pl.pallas_call(
        _mlp_kernel,
        grid=(b, s // tm),
        in_specs=[tok(d), tok(attn.shape[2]), tok(conv.shape[2]),
                  pl.BlockSpec((1,) + mod.shape[1:], lambda bi, i: (bi, 0, 0)),
                  _const_spec(wo_a.shape), _const_spec(wo_c.shape),
                  _const_spec(w1.shape), _const_spec(w2.shape), _const_spec(gf.shape)],
        out_specs=tok(d),
        out_shape=jax.ShapeDtypeStruct((b, s, d), F32),
        compiler_params=pltpu.CompilerParams(dimension_semantics=("arbitrary", "arbitrary"),
                                             vmem_limit_bytes=_Plan.vmem_limit),
        name="out_proj_mlp",
    )(x, attn, conv, mod, wo_a, wo_c, w1, w2, gf)


def _rope_lane_order():
    half = QK_ROPE_DIM // 4
    x1 = np.concatenate([np.arange(half), 2 * half + np.arange(half)])
    x2 = x1 + half
    return np.concatenate([x1, x2, x1, x2])


def _rope_tables(rows, scale):
    half = QK_ROPE_DIM // 4
    pos = jnp.arange(rows * GRID_W)
    freqs = ROPE_THETA ** (-jnp.arange(0, 2 * half, 2, dtype=F32) / (2 * half))
    ang = jnp.concatenate([(pos // GRID_W).astype(F32)[:, None] * freqs,
                           (pos % GRID_W).astype(F32)[:, None] * freqs], axis=1)
    cos, sin = jnp.cos(ang), jnp.sin(ang)
    zeros = lambda w: jnp.zeros((pos.shape[0], w), F32)
    pad = LANES - QK_NOPE_DIM - 2 * ROPE_HALF
    cos_k = jnp.concatenate([zeros(QK_NOPE_DIM), cos, cos, zeros(pad)], axis=1)
    sin_k = jnp.concatenate([zeros(QK_NOPE_DIM), -sin, sin, zeros(pad)], axis=1)
    cos_q = jnp.concatenate([jnp.ones((pos.shape[0], QK_NOPE_DIM), F32), cos, cos, zeros(pad)], axis=1)
    return cos_q * scale, sin_k * scale, cos_k, sin_k


def kernel(x, c, ctx, c_ctx, w_mod, b_mod, w_in, q_norm_g, w_uq, kv_norm_g, w_ukv,
           conv_w, w_out, w_mlp1, w_mlp2, final_norm_g):
    b, s, d = x.shape
    l = ctx.shape[1]
    depth = w_mod.shape[0]
    assert depth == 1, "single-layer configuration"
    assert s % _Plan.tok_tile == 0 and s % _Plan.q_tile == 0 and s % _Plan.kv_chunk == 0
    assert s % GRID_W == 0 and l % LANES == 0 and N_HEADS % HEADS_PER_STEP == 0
    cw = conv_w.shape[2]
    assert w_in.shape[2] == MLA_IN + 3 * cw and d == N_HEADS * V_HEAD_DIM + cw

    ctx_row = b
    rows = -(-(b + 1) // SUBLANES_F32) * SUBLANES_F32
    cvec = jnp.concatenate([c, c_ctx[None, :], jnp.zeros((rows - b - 1, d), F32)], axis=0)
    mod = _adaln(cvec, w_mod[0], b_mod[0][None, :]).reshape(rows, 6, d)

    order = _rope_lane_order()
    pad64 = jnp.zeros((d, LANES - len(order)), F32)
    wi = w_in[0]
    w_in_p = jnp.concatenate(
        [wi[:, :Q_LORA_RANK + KV_LORA_RANK], pad64, wi[:, Q_LORA_RANK + KV_LORA_RANK + order], wi[:, MLA_IN:]],
        axis=1).astype(BF16)
    wq = w_uq[0].reshape(Q_LORA_RANK, N_HEADS, QK_DIM)
    w_uq_p = jnp.concatenate([wq[:, :, :QK_NOPE_DIM], wq[:, :, QK_NOPE_DIM + order]], axis=2)
    w_uq_p = w_uq_p.reshape(Q_LORA_RANK, N_HEADS * HEAD_SLOT).astype(BF16)
    wkv = w_ukv[0].reshape(KV_LORA_RANK, N_HEADS, QK_NOPE_DIM + V_HEAD_DIM)
    wk = jnp.concatenate([wkv[:, :, :QK_NOPE_DIM],
                          jnp.zeros((KV_LORA_RANK, N_HEADS, HEAD_SLOT - QK_NOPE_DIM), F32)], axis=2)
    w_ukv_p = jnp.concatenate([wk.reshape(KV_LORA_RANK, N_HEADS * HEAD_SLOT),
                               wkv[:, :, QK_NOPE_DIM:].reshape(KV_LORA_RANK, N_HEADS * V_HEAD_DIM)],
                              axis=1).astype(BF16)

    cos_q, sin_q, cos_k, sin_k = _rope_tables(s // GRID_W, ATTN_SCALE * LOG2E)
    lane = jnp.arange(LANES)
    ctx_cos = jnp.broadcast_to(((lane >= QK_NOPE_DIM) & (lane < QK_DIM)).astype(F32), (l, LANES))
    ctx_sin = jnp.zeros((l, LANES), F32)

    q, k, v, conv = _project_tokens(x, mod, w_in_p, q_norm_g, w_uq_p, kv_norm_g, w_ukv_p, conv_w[0],
                                    (cos_q, sin_q, cos_k, sin_k))
    kc, vc = _project_ctx(ctx, mod, ctx_row, w_in_p, kv_norm_g, w_ukv_p, (ctx_cos, ctx_sin))
    attn = _attention(q, k, v, kc, vc)

    nv = N_HEADS * V_HEAD_DIM
    wo = w_out[0].astype(BF16)
    return _out_proj_mlp(x, attn, conv, mod, wo[:nv], wo[nv:], w_mlp1[0].astype(BF16),
                         w_mlp2[0].astype(BF16), final_norm_g[None, :])
```

```python
import functools
import math

import jax
import jax.numpy as jnp
import numpy as np
from jax import lax
from jax.experimental import pallas as pl
from jax.experimental.pallas import tpu as pltpu

GRID_W = 64
N_HEADS = 8
QK_NOPE_DIM = 64
QK_ROPE_DIM = 32
V_HEAD_DIM = 64
Q_LORA_RANK = 256
KV_LORA_RANK = 128
CONV_K = 3
ROPE_THETA = 10000.0
EPS = 1e-6
QK_DIM = QK_NOPE_DIM + QK_ROPE_DIM
ROPE_HALF = QK_ROPE_DIM // 2
MLA_IN = Q_LORA_RANK + KV_LORA_RANK + QK_ROPE_DIM
ATTN_SCALE = 1.0 / math.sqrt(QK_DIM)
LOG2E = math.log2(math.e)

LANES = 128
SUBLANES_F32 = 8
VMEM_BYTES_V7X = 64 * 1024 * 1024

HEAD_SLOT = LANES
HEADS_PER_STEP = 2
HALO = SUBLANES_F32

F32 = jnp.float32
BF16 = jnp.bfloat16


class _Plan:
    tok_tile = 512
    q_tile = 1024
    kv_chunk = 512
    ff_chunk = 1024
    mod_cols = 1536
    vmem_limit = 56 * 1024 * 1024
    assert vmem_limit < VMEM_BYTES_V7X


def _rms(x):
    return x * lax.rsqrt(jnp.mean(x * x, axis=-1, keepdims=True) + EPS)


def _dot(a, b):
    return jnp.dot(a, b, preferred_element_type=F32)


def _dot_nt(a, b):
    return lax.dot_general(a, b, (((1,), (1,)), ((), ())), preferred_element_type=F32)


def _rope(t, cos_tab, sin_tab):
    return t * cos_tab + pltpu.roll(t, LANES - ROPE_HALF, 1) * sin_tab


def _adaln_kernel(c_ref, w_ref, b_ref, o_ref):
    c = c_ref[...]
    a = (c / (1.0 + jnp.exp(-c))).astype(BF16)
    o_ref[...] = _dot(a, w_ref[...].astype(BF16)) + b_ref[...]


def _adaln(cvec, w_mod, b_mod):
    rows, d = cvec.shape
    n = w_mod.shape[1]
    tn = _Plan.mod_cols
    return pl.pallas_call(
        _adaln_kernel,
        grid=(n // tn,),
        in_specs=[pl.BlockSpec((rows, d), lambda j: (0, 0)),
                  pl.BlockSpec((d, tn), lambda j: (0, j)),
                  pl.BlockSpec((1, tn), lambda j: (0, j))],
        out_specs=pl.BlockSpec((rows, tn), lambda j: (0, j)),
        out_shape=jax.ShapeDtypeStruct((rows, n), F32),
        compiler_params=pltpu.CompilerParams(dimension_semantics=("arbitrary",),
                                             vmem_limit_bytes=_Plan.vmem_limit),
        name="adaln",
    )(cvec, w_mod, b_mod)


def _kv_from_z(z_ckv, z_rope, kvg_ref, wukv_ref, ck_ref, sk_ref, k_ref, v_ref):
    ckv = _rms(z_ckv) * kvg_ref[...]
    kv = _dot(ckv.astype(BF16), wukv_ref[...])
    k_rope = _rope(z_rope, ck_ref[...], sk_ref[...])
    for h in range(N_HEADS):
        sl = slice(h * HEAD_SLOT, (h + 1) * HEAD_SLOT)
        k_ref[0, :, sl] = (kv[:, sl] + k_rope).astype(BF16)
    v_ref[0] = kv[:, N_HEADS * HEAD_SLOT:].astype(BF16)


def _proj_kernel(xp_ref, x_ref, xn_ref, mod_ref, win_ref, qg_ref, wuq_ref, kvg_ref, wukv_ref, cw_ref,
                 cq_ref, sq_ref, ck_ref, sk_ref, q_ref, k_ref, v_ref, y_ref, u_scr):
    i = pl.program_id(1)
    n = pl.num_programs(1)
    tm = x_ref.shape[1]
    cw = y_ref.shape[2]
    shift, scale = mod_ref[0, 0:1, :], mod_ref[0, 1:2, :]
    xe = jnp.concatenate([xp_ref[0], x_ref[0], xn_ref[0]], axis=0)
    hmod = (_rms(xe) * (1.0 + scale) + shift).astype(BF16)
    z = _dot(hmod, win_ref[...])
    zm = z[HALO:HALO + tm]

    cq = _rms(zm[:, :Q_LORA_RANK]) * qg_ref[...]
    qf = _dot(cq.astype(BF16), wuq_ref[...])
    cos_q, sin_q = cq_ref[...], sq_ref[...]
    for h in range(N_HEADS):
        sl = slice(h * HEAD_SLOT, (h + 1) * HEAD_SLOT)
        q_ref[0, :, sl] = _rope(qf[:, sl], cos_q, sin_q).astype(BF16)

    c0 = Q_LORA_RANK
    c1 = c0 + KV_LORA_RANK
    c2 = c1 + LANES
    _kv_from_z(zm[:, c0:c1], zm[:, c1:c2], kvg_ref, wukv_ref, ck_ref, sk_ref, k_ref, v_ref)

    gate_b = zm[:, c2:c2 + cw]
    u = z[:, c2 + cw:c2 + 2 * cw] * z[:, c2 + 2 * cw:c2 + 3 * cw]
    row = lax.broadcasted_iota(jnp.int32, (tm + 2 * HALO, 1), 0)
    keep = ((row >= HALO) | (i > 0)) & ((row < HALO + tm) | (i < n - 1))
    u_scr[...] = jnp.where(keep, u, 0.0)
    w = cw_ref[...]
    y = (w[0:1] * u_scr[HALO - 1:HALO - 1 + tm]
         + w[1:2] * u_scr[HALO:HALO + tm]
         + w[2:3] * u_scr[HALO + 1:HALO + 1 + tm])
    y_ref[0] = (gate_b * y).astype(BF16)


def _ctx_kv_kernel(x_ref, mod_ref, win_ref, kvg_ref, wukv_ref, ck_ref, sk_ref, k_ref, v_ref):
    shift, scale = mod_ref[0, 0:1, :], mod_ref[0, 1:2, :]
    hmod = (_rms(x_ref[0]) * (1.0 + scale) + shift).astype(BF16)
    z = _dot(hmod, win_ref[...])
    _kv_from_z(z[:, :KV_LORA_RANK], z[:, KV_LORA_RANK:], kvg_ref, wukv_ref, ck_ref, sk_ref, k_ref, v_ref)


def _const_spec(shape):
    return pl.BlockSpec(shape, lambda *_: (0,) * len(shape))


def _project_tokens(x, mod, w_in_p, q_g, w_uq_p, kv_g, w_ukv_p, conv_w, tabs):
    b, s, d = x.shape
    tm = _Plan.tok_tile
    cw = conv_w.shape[1]
    nh = N_HEADS * HEAD_SLOT
    nv = N_HEADS * V_HEAD_DIM
    halo_blocks = tm // HALO
    last_halo = s // HALO - 1
    tab_spec = pl.BlockSpec((tm, LANES), lambda bi, i: (i, 0))
    tok = lambda w: pl.BlockSpec((1, tm, w), lambda bi, i: (bi, i, 0))
    return pl.pallas_call(
        _proj_kernel,
        grid=(b, s // tm),
        in_specs=[
            pl.BlockSpec((1, HALO, d), lambda bi, i: (bi, jnp.maximum(i * halo_blocks - 1, 0), 0)),
            tok(d),
            pl.BlockSpec((1, HALO, d), lambda bi, i: (bi, jnp.minimum((i + 1) * halo_blocks, last_halo), 0)),
            pl.BlockSpec((1,) + mod.shape[1:], lambda bi, i: (bi, 0, 0)),
            _const_spec(w_in_p.shape), _const_spec(q_g.shape), _const_spec(w_uq_p.shape),
            _const_spec(kv_g.shape), _const_spec(w_ukv_p.shape), _const_spec(conv_w.shape),
            tab_spec, tab_spec, tab_spec, tab_spec,
        ],
        out_specs=[tok(nh), tok(nh), tok(nv), tok(cw)],
        out_shape=[jax.ShapeDtypeStruct((b, s, nh), BF16), jax.ShapeDtypeStruct((b, s, nh), BF16),
                   jax.ShapeDtypeStruct((b, s, nv), BF16), jax.ShapeDtypeStruct((b, s, cw), BF16)],
        scratch_shapes=[pltpu.VMEM((tm + 2 * HALO, cw), F32)],
        compiler_params=pltpu.CompilerParams(dimension_semantics=("arbitrary", "arbitrary"),
                                             vmem_limit_bytes=_Plan.vmem_limit),
        name="token_proj",
    )(x, x, x, mod, w_in_p, q_g, w_uq_p, kv_g, w_ukv_p, conv_w, *tabs)


def _project_ctx(ctx, mod, ctx_row, w_in_p, kv_g, w_ukv_p, tabs):
    b, l, d = ctx.shape
    nh = N_HEADS * HEAD_SLOT
    nv = N_HEADS * V_HEAD_DIM
    kv_cols = 2 * LANES
    tok = lambda w: pl.BlockSpec((1, l, w), lambda bi: (bi, 0, 0))
    return pl.pallas_call(
        _ctx_kv_kernel,
        grid=(b,),
        in_specs=[
            tok(d),
            pl.BlockSpec((1,) + mod.shape[1:], lambda bi: (ctx_row, 0, 0)),
            pl.BlockSpec((d, kv_cols), lambda bi: (0, Q_LORA_RANK // kv_cols)),
            _const_spec(kv_g.shape), _const_spec(w_ukv_p.shape),
            _const_spec(tabs[0].shape), _const_spec(tabs[1].shape),
        ],
        out_specs=[tok(nh), tok(nv)],
        out_shape=[jax.ShapeDtypeStruct((b, l, nh), BF16), jax.ShapeDtypeStruct((b, l, nv), BF16)],
        compiler_params=pltpu.CompilerParams(dimension_semantics=("arbitrary",),
                                             vmem_limit_bytes=_Plan.vmem_limit),
        name="ctx_proj",
    )(ctx, mod, w_in_p, kv_g, w_ukv_p, *tabs)


def _lane_tile_reduce(op, a):
    return functools.reduce(op, [a[:, t * LANES:(t + 1) * LANES] for t in range(a.shape[1] // LANES)])


def _key_chunks(main_ref, ctx_ref):
    s_len, tk = main_ref.shape[1], _Plan.kv_chunk
    return [(main_ref, c * tk, tk, c * tk) for c in range(s_len // tk)] + [(ctx_ref, 0, ctx_ref.shape[1], s_len)]


def _scores(q_ref, k_ref, kc_ref, head, s_scr):
    hs = slice(head * HEAD_SLOT, (head + 1) * HEAD_SLOT)
    q = q_ref[0, :, hs]
    m_run = None
    for ref, r0, rows, c0 in _key_chunks(k_ref, kc_ref):
        s = _dot_nt(q, ref[0, r0:r0 + rows, hs])
        s_scr[:, c0:c0 + rows] = s
        cm = _lane_tile_reduce(jnp.maximum, s)
        m_run = cm if m_run is None else jnp.maximum(m_run, cm)
    return jnp.broadcast_to(jnp.max(m_run, axis=-1, keepdims=True), m_run.shape)


def _weighted_values(s_scr, m_rep, v_ref, vc_ref):
    acc = None
    for ref, r0, rows, c0 in _key_chunks(v_ref, vc_ref):
        tiles = [jnp.exp2(s_scr[:, c0 + t * LANES:c0 + (t + 1) * LANES] - m_rep) for t in range(rows // LANES)]
        rhs = jnp.concatenate([ref[0, r0:r0 + rows, :], jnp.ones((rows, LANES), BF16)], axis=1)
        pv = _dot(jnp.concatenate(tiles, axis=1).astype(BF16), rhs)
        acc = pv if acc is None else acc + pv
    return acc


def _attn_kernel(q_ref, k_ref, kc_ref, vp_ref, vcp_ref, vn_ref, vcn_ref, o_ref,
                 s0_scr, s1_scr, m0_scr, m1_scr, o0_scr):
    t = pl.program_id(0)

    @pl.when(t == 0)
    def _():
        s1_scr[...] = jnp.zeros_like(s1_scr)
        m1_scr[...] = jnp.zeros_like(m1_scr)
        o0_scr[...] = jnp.ones_like(o0_scr)

    @pl.when(t >= 0)
    def _():
        m0_scr[...] = _scores(q_ref, k_ref, kc_ref, 0, s0_scr)
        acc1 = _weighted_values(s1_scr, m1_scr[...], vp_ref, vcp_ref)
        acc0 = o0_scr[...]
        o0 = acc0[:, :LANES] / acc0[:, LANES:]
        o1 = acc1[:, :LANES] / acc1[:, LANES:]
        lane = lax.broadcasted_iota(jnp.int32, o0.shape, 1)
        o_ref[0] = jnp.where(lane < V_HEAD_DIM, o0, o1).astype(BF16)

    @pl.when(t < pl.num_programs(0))
    def _():
        m1_scr[...] = _scores(q_ref, k_ref, kc_ref, 1, s1_scr)
        o0_scr[...] = _weighted_values(s0_scr, m0_scr[...], vn_ref, vcn_ref)


def _attention(q, k, v, kc, vc):
    b, s, _ = q.shape
    l = kc.shape[1]
    tq = _Plan.q_tile
    groups = N_HEADS // HEADS_PER_STEP
    n_q = s // tq
    tiles = b * groups * n_q
    pair_w = HEADS_PER_STEP * HEAD_SLOT
    o_w = HEADS_PER_STEP * V_HEAD_DIM

    def decode(u):
        return u // (groups * n_q), (u // n_q) % groups, u % n_q

    def cur(t):
        return decode(jnp.minimum(t, tiles - 1))

    def prev(t):
        return decode(jnp.maximum(t - 1, 0))

    def keys_of(which, rows, width):
        return pl.BlockSpec((1, rows, width), lambda t: (which(t)[0], 0, which(t)[1]))

    def rows_of(which, width):
        return pl.BlockSpec((1, tq, width), lambda t: (which(t)[0], which(t)[2], which(t)[1]))

    return pl.pallas_call(
        _attn_kernel,
        grid=(tiles + 1,),
        in_specs=[rows_of(cur, pair_w), keys_of(cur, s, pair_w), keys_of(cur, l, pair_w),
                  keys_of(prev, s, o_w), keys_of(prev, l, o_w), keys_of(cur, s, o_w), keys_of(cur, l, o_w)],
        out_specs=rows_of(prev, o_w),
        out_shape=jax.ShapeDtypeStruct((b, s, N_HEADS * V_HEAD_DIM), BF16),
        scratch_shapes=[pltpu.VMEM((tq, s + l), F32), pltpu.VMEM((tq, s + l), F32),
                        pltpu.VMEM((tq, LANES), F32), pltpu.VMEM((tq, LANES), F32),
                        pltpu.VMEM((tq, 2 * LANES), F32)],
        compiler_params=pltpu.CompilerParams(dimension_semantics=("arbitrary",),
                                             vmem_limit_bytes=_Plan.vmem_limit),
        name="attention",
    )(q, k, kc, v, vc, v, vc)


def _mlp_kernel(x_ref, a_ref, y_ref, mod_ref, woa_ref, woc_ref, w1_ref, w2_ref, gf_ref, o_ref):
    gate1 = mod_ref[0, 2:3, :]
    shift2, scale2, gate2 = mod_ref[0, 3:4, :], mod_ref[0, 4:5, :], mod_ref[0, 5:6, :]
    mix = _dot(a_ref[0], woa_ref[...]) + _dot(y_ref[0], woc_ref[...])
    x1 = x_ref[0] + gate1 * mix
    hmod = (_rms(x1) * (1.0 + scale2) + shift2).astype(BF16)
    fc = _Plan.ff_chunk
    acc = None
    for j in range(w1_ref.shape[1] // fc):
        u = jnp.maximum(_dot(hmod, w1_ref[:, j * fc:(j + 1) * fc]), 0.0)
        part = _dot((u * u).astype(BF16), w2_ref[j * fc:(j + 1) * fc, :])
        acc = part if acc is None else acc + part
    x2 = x1 + gate2 * acc
    o_ref[0] = _rms(x2) * gf_ref[...]


def _out_proj_mlp(x, attn, conv, mod, wo_a, wo_c, w1, w2, gf):
    b, s, d = x.shape
    tm = _Plan.tok_tile
    tok = lambda w: pl.BlockSpec((1, tm, w), lambda bi, i: (bi, i, 0))
    return pl.pallas_call(
        _mlp_kernel,
        grid=(b, s // tm),
        in_specs=[tok(d), tok(attn.shape[2]), tok(conv.shape[2]),
                  pl.BlockSpec((1,) + mod.shape[1:], lambda bi, i: (bi, 0, 0)),
                  _const_spec(wo_a.shape), _const_spec(wo_c.shape),
                  _const_spec(w1.shape), _const_spec(w2.shape), _const_spec(gf.shape)],
        out_specs=tok(d),
        out_shape=jax.ShapeDtypeStruct((b, s, d), F32),
        compiler_params=pltpu.CompilerParams(dimension_semantics=("arbitrary", "arbitrary"),
                                             vmem_limit_bytes=_Plan.vmem_limit),
        name="out_proj_mlp",
    )(x, attn, conv, mod, wo_a, wo_c, w1, w2, gf)


def _rope_lane_order():
    half = QK_ROPE_DIM // 4
    x1 = np.concatenate([np.arange(half), 2 * half + np.arange(half)])
    x2 = x1 + half
    return np.concatenate([x1, x2, x1, x2])


def _rope_tables(rows, scale):
    half = QK_ROPE_DIM // 4
    f32 = np.float32
    pos = np.arange(rows * GRID_W)
    freqs = f32(ROPE_THETA) ** (-np.arange(0, 2 * half, 2, dtype=f32) / f32(2 * half))
    ang = np.concatenate([(pos // GRID_W).astype(f32)[:, None] * freqs,
                          (pos % GRID_W).astype(f32)[:, None] * freqs], axis=1)
    cos, sin = np.cos(ang), np.sin(ang)
    zeros = lambda w: np.zeros((pos.shape[0], w), f32)
    pad = LANES - QK_NOPE_DIM - 2 * ROPE_HALF
    cos_k = np.concatenate([zeros(QK_NOPE_DIM), cos, cos, zeros(pad)], axis=1)
    sin_k = np.concatenate([zeros(QK_NOPE_DIM), -sin, sin, zeros(pad)], axis=1)
    cos_q = np.concatenate([np.ones((pos.shape[0], QK_NOPE_DIM), f32), cos, cos, zeros(pad)], axis=1)
    return (cos_q * f32(scale)).astype(f32), (sin_k * f32(scale)).astype(f32), cos_k, sin_k


def kernel(x, c, ctx, c_ctx, w_mod, b_mod, w_in, q_norm_g, w_uq, kv_norm_g, w_ukv,
           conv_w, w_out, w_mlp1, w_mlp2, final_norm_g):
    b, s, d = x.shape
    l = ctx.shape[1]
    depth = w_mod.shape[0]
    assert depth == 1, "single-layer configuration"
    assert s % _Plan.tok_tile == 0 and s % _Plan.q_tile == 0 and s % _Plan.kv_chunk == 0
    assert s % GRID_W == 0 and l % LANES == 0 and N_HEADS % HEADS_PER_STEP == 0
    cw = conv_w.shape[2]
    assert w_in.shape[2] == MLA_IN + 3 * cw and d == N_HEADS * V_HEAD_DIM + cw

    ctx_row = b
    rows = -(-(b + 1) // SUBLANES_F32) * SUBLANES_F32
    cvec = jnp.concatenate([c, c_ctx[None, :], jnp.zeros((rows - b - 1, d), F32)], axis=0)
    mod = _adaln(cvec, w_mod[0], b_mod[0][None, :]).reshape(rows, 6, d)

    order = _rope_lane_order()
    pad64 = jnp.zeros((d, LANES - len(order)), F32)
    wi = w_in[0]
    w_in_p = jnp.concatenate(
        [wi[:, :Q_LORA_RANK + KV_LORA_RANK], pad64, wi[:, Q_LORA_RANK + KV_LORA_RANK + order], wi[:, MLA_IN:]],
        axis=1).astype(BF16)
    wq = w_uq[0].reshape(Q_LORA_RANK, N_HEADS, QK_DIM)
    w_uq_p = jnp.concatenate([wq[:, :, :QK_NOPE_DIM], wq[:, :, QK_NOPE_DIM + order]], axis=2)
    w_uq_p = w_uq_p.reshape(Q_LORA_RANK, N_HEADS * HEAD_SLOT).astype(BF16)
    wkv = w_ukv[0].reshape(KV_LORA_RANK, N_HEADS, QK_NOPE_DIM + V_HEAD_DIM)
    wk = jnp.concatenate([wkv[:, :, :QK_NOPE_DIM],
                          jnp.zeros((KV_LORA_RANK, N_HEADS, HEAD_SLOT - QK_NOPE_DIM), F32)], axis=2)
    w_ukv_p = jnp.concatenate([wk.reshape(KV_LORA_RANK, N_HEADS * HEAD_SLOT),
                               wkv[:, :, QK_NOPE_DIM:].reshape(KV_LORA_RANK, N_HEADS * V_HEAD_DIM)],
                              axis=1).astype(BF16)

    cos_q, sin_q, cos_k, sin_k = _rope_tables(s // GRID_W, ATTN_SCALE * LOG2E)
    lane = np.arange(LANES)
    ctx_cos = np.broadcast_to(((lane >= QK_NOPE_DIM) & (lane < QK_DIM)).astype(np.float32), (l, LANES))
    ctx_sin = np.zeros((l, LANES), np.float32)

    q, k, v, conv = _project_tokens(x, mod, w_in_p, q_norm_g, w_uq_p, kv_norm_g, w_ukv_p, conv_w[0],
                                    (cos_q, sin_q, cos_k, sin_k))
    kc, vc = _project_ctx(ctx, mod, ctx_row, w_in_p, kv_norm_g, w_ukv_p, (ctx_cos, ctx_sin))
    attn = _attention(q, k, v, kc, vc)

    nv = N_HEADS * V_HEAD_DIM
    wo = w_out[0].astype(BF16)
    return _out_proj_mlp(x, attn, conv, mod, wo[:nv], wo[nv:], w_mlp1[0].astype(BF16),
                         w_mlp2[0].astype(BF16), final_norm_g[None, :])
```

```python
import functools
import math

import jax
import jax.numpy as jnp
import numpy as np
from jax import lax
from jax.experimental import pallas as pl
from jax.experimental.pallas import tpu as pltpu

GRID_W = 64
N_HEADS = 8
QK_NOPE_DIM = 64
QK_ROPE_DIM = 32
V_HEAD_DIM = 64
Q_LORA_RANK = 256
KV_LORA_RANK = 128
CONV_K = 3
ROPE_THETA = 10000.0
EPS = 1e-6
QK_DIM = QK_NOPE_DIM + QK_ROPE_DIM
ROPE_HALF = QK_ROPE_DIM // 2
MLA_IN = Q_LORA_RANK + KV_LORA_RANK + QK_ROPE_DIM
ATTN_SCALE = 1.0 / math.sqrt(QK_DIM)
LOG2E = math.log2(math.e)

LANES = 128
SUBLANES_F32 = 8
VMEM_BYTES_V7X = 64 * 1024 * 1024

HEAD_SLOT = LANES
HEADS_PER_STEP = 2
HALO = SUBLANES_F32

F32 = jnp.float32
BF16 = jnp.bfloat16


class _Plan:
    tok_tile = 512
    mlp_tile = 1024
    mlp_sub_tiles = 2
    q_tile = 1024
    kv_chunk = 512
    ff_chunk = 1024
    mod_cols = 1536
    vmem_limit = 56 * 1024 * 1024
    assert vmem_limit < VMEM_BYTES_V7X


def _rms(x):
    return x * lax.rsqrt(jnp.mean(x * x, axis=-1, keepdims=True) + EPS)


def _dot(a, b):
    return jnp.dot(a, b, preferred_element_type=F32)


def _dot_nt(a, b):
    return lax.dot_general(a, b, (((1,), (1,)), ((), ())), preferred_element_type=F32)


def _rope(t, cos_tab, sin_tab):
    return t * cos_tab + pltpu.roll(t, LANES - ROPE_HALF, 1) * sin_tab


def _adaln_kernel(c_ref, w_ref, b_ref, o_ref):
    c = c_ref[...]
    a = (c / (1.0 + jnp.exp(-c))).astype(BF16)
    o_ref[...] = _dot(a, w_ref[...].astype(BF16)) + b_ref[...]


def _adaln(cvec, w_mod, b_mod):
    rows, d = cvec.shape
    n = w_mod.shape[1]
    tn = _Plan.mod_cols
    return pl.pallas_call(
        _adaln_kernel,
        grid=(n // tn,),
        in_specs=[pl.BlockSpec((rows, d), lambda j: (0, 0)),
                  pl.BlockSpec((d, tn), lambda j: (0, j)),
                  pl.BlockSpec((1, tn), lambda j: (0, j))],
        out_specs=pl.BlockSpec((rows, tn), lambda j: (0, j)),
        out_shape=jax.ShapeDtypeStruct((rows, n), F32),
        compiler_params=pltpu.CompilerParams(dimension_semantics=("arbitrary",),
                                             vmem_limit_bytes=_Plan.vmem_limit),
        name="adaln",
    )(cvec, w_mod, b_mod)


def _kv_from_z(z_ckv, z_rope, kvg_ref, wukv_ref, ck_ref, sk_ref, k_ref, v_ref):
    ckv = _rms(z_ckv) * kvg_ref[...]
    kv = _dot(ckv.astype(BF16), wukv_ref[...])
    k_rope = _rope(z_rope, ck_ref[...], sk_ref[...])
    for h in range(N_HEADS):
        sl = slice(h * HEAD_SLOT, (h + 1) * HEAD_SLOT)
        k_ref[0, :, sl] = (kv[:, sl] + k_rope).astype(BF16)
    v_ref[0] = kv[:, N_HEADS * HEAD_SLOT:].astype(BF16)


def _proj_kernel(xp_ref, x_ref, xn_ref, mod_ref, wmla_ref, wconv_ref, qg_ref, wuq_ref, kvg_ref, wukv_ref, cw_ref,
                 cq_ref, sq_ref, ck_ref, sk_ref, q_ref, k_ref, v_ref, y_ref, u_scr):
    i = pl.program_id(1)
    n = pl.num_programs(1)
    tm = x_ref.shape[1]
    cw = y_ref.shape[2]
    shift, scale = mod_ref[0, 0:1, :], mod_ref[0, 1:2, :]
    xe = jnp.concatenate([xp_ref[0], x_ref[0], xn_ref[0]], axis=0)
    hmod = (_rms(xe) * (1.0 + scale) + shift).astype(BF16)
    c0 = Q_LORA_RANK
    c1 = c0 + KV_LORA_RANK
    c2 = c1 + LANES
    main = slice(HALO, HALO + tm)
    z_mla = _dot(hmod, wmla_ref[...])[main]
    z_conv = _dot(hmod, wconv_ref[:, cw:])

    cq = _rms(z_mla[:, :c0]) * qg_ref[...]
    qf = _dot(cq.astype(BF16), wuq_ref[...])

    u = z_conv[:, :cw] * z_conv[:, cw:]
    row = lax.broadcasted_iota(jnp.int32, (tm + 2 * HALO, 1), 0)
    keep = ((row >= HALO) | (i > 0)) & ((row < HALO + tm) | (i < n - 1))
    u_scr[...] = jnp.where(keep, u, 0.0)
    w = cw_ref[...]
    y = (w[0:1] * u_scr[HALO - 1:HALO - 1 + tm]
         + w[1:2] * u_scr[HALO:HALO + tm]
         + w[2:3] * u_scr[HALO + 1:HALO + 1 + tm])

    cos_q, sin_q = cq_ref[...], sq_ref[...]
    for h in range(N_HEADS):
        sl = slice(h * HEAD_SLOT, (h + 1) * HEAD_SLOT)
        q_ref[0, :, sl] = _rope(qf[:, sl], cos_q, sin_q).astype(BF16)
    _kv_from_z(z_mla[:, c0:c1], z_mla[:, c1:c2], kvg_ref, wukv_ref, ck_ref, sk_ref, k_ref, v_ref)

    gate_b = _dot(hmod, wconv_ref[:, :cw])[main]
    y_ref[0] = (gate_b * y).astype(BF16)


def _ctx_kv_kernel(x_ref, mod_ref, win_ref, kvg_ref, wukv_ref, ck_ref, sk_ref, k_ref, v_ref):
    shift, scale = mod_ref[0, 0:1, :], mod_ref[0, 1:2, :]
    hmod = (_rms(x_ref[0]) * (1.0 + scale) + shift).astype(BF16)
    z = _dot(hmod, win_ref[...])
    _kv_from_z(z[:, :KV_LORA_RANK], z[:, KV_LORA_RANK:], kvg_ref, wukv_ref, ck_ref, sk_ref, k_ref, v_ref)


def _const_spec(shape):
    return pl.BlockSpec(shape, lambda *_: (0,) * len(shape))


def _project_tokens(x, mod, w_mla, w_conv, q_g, w_uq_p, kv_g, w_ukv_p, conv_w, tabs):
    b, s, d = x.shape
    tm = _Plan.tok_tile
    cw = conv_w.shape[1]
    nh = N_HEADS * HEAD_SLOT
    nv = N_HEADS * V_HEAD_DIM
    halo_blocks = tm // HALO
    last_halo = s // HALO - 1
    tab_spec = pl.BlockSpec((tm, LANES), lambda bi, i: (i, 0))
    tok = lambda w: pl.BlockSpec((1, tm, w), lambda bi, i: (bi, i, 0))
    return pl.pallas_call(
        _proj_kernel,
        grid=(b, s // tm),
        in_specs=[
            pl.BlockSpec((1, HALO, d), lambda bi, i: (bi, jnp.maximum(i * halo_blocks - 1, 0), 0)),
            tok(d),
            pl.BlockSpec((1, HALO, d), lambda bi, i: (bi, jnp.minimum((i + 1) * halo_blocks, last_halo), 0)),
            pl.BlockSpec((1,) + mod.shape[1:], lambda bi, i: (bi, 0, 0)),
            _const_spec(w_mla.shape), _const_spec(w_conv.shape), _const_spec(q_g.shape), _const_spec(w_uq_p.shape),
            _const_spec(kv_g.shape), _const_spec(w_ukv_p.shape), _const_spec(conv_w.shape),
            tab_spec, tab_spec, tab_spec, tab_spec,
        ],
        out_specs=[tok(nh), tok(nh), tok(nv), tok(cw)],
        out_shape=[jax.ShapeDtypeStruct((b, s, nh), BF16), jax.ShapeDtypeStruct((b, s, nh), BF16),
                   jax.ShapeDtypeStruct((b, s, nv), BF16), jax.ShapeDtypeStruct((b, s, cw), BF16)],
        scratch_shapes=[pltpu.VMEM((tm + 2 * HALO, cw), F32)],
        compiler_params=pltpu.CompilerParams(dimension_semantics=("arbitrary", "arbitrary"),
                                             vmem_limit_bytes=_Plan.vmem_limit),
        name="token_proj",
    )(x, x, x, mod, w_mla, w_conv, q_g, w_uq_p, kv_g, w_ukv_p, conv_w, *tabs)


def _project_ctx(ctx, mod, ctx_row, w_mla, kv_g, w_ukv_p, tabs):
    b, l, d = ctx.shape
    nh = N_HEADS * HEAD_SLOT
    nv = N_HEADS * V_HEAD_DIM
    kv_cols = 2 * LANES
    tok = lambda w: pl.BlockSpec((1, l, w), lambda bi: (bi, 0, 0))
    return pl.pallas_call(
        _ctx_kv_kernel,
        grid=(b,),
        in_specs=[
            tok(d),
            pl.BlockSpec((1,) + mod.shape[1:], lambda bi: (ctx_row, 0, 0)),
            pl.BlockSpec((d, kv_cols), lambda bi: (0, Q_LORA_RANK // kv_cols)),
            _const_spec(kv_g.shape), _const_spec(w_ukv_p.shape),
            _const_spec(tabs[0].shape), _const_spec(tabs[1].shape),
        ],
        out_specs=[tok(nh), tok(nv)],
        out_shape=[jax.ShapeDtypeStruct((b, l, nh), BF16), jax.ShapeDtypeStruct((b, l, nv), BF16)],
        compiler_params=pltpu.CompilerParams(dimension_semantics=("arbitrary",),
                                             vmem_limit_bytes=_Plan.vmem_limit),
        name="ctx_proj",
    )(ctx, mod, w_mla, kv_g, w_ukv_p, *tabs)


def _lane_tile_reduce(op, a):
    return functools.reduce(op, [a[:, t * LANES:(t + 1) * LANES] for t in range(a.shape[1] // LANES)])


def _key_chunks(main_ref, ctx_ref):
    s_len, tk = main_ref.shape[1], _Plan.kv_chunk
    return [(main_ref, c * tk, tk, c * tk) for c in range(s_len // tk)] + [(ctx_ref, 0, ctx_ref.shape[1], s_len)]


def _scores(q_ref, k_ref, kc_ref, head, s_scr):
    hs = slice(head * HEAD_SLOT, (head + 1) * HEAD_SLOT)
    q = q_ref[0, :, hs]
    m_run = None
    for ref, r0, rows, c0 in _key_chunks(k_ref, kc_ref):
        s = _dot_nt(q, ref[0, r0:r0 + rows, hs])
        s_scr[:, c0:c0 + rows] = s
        cm = _lane_tile_reduce(jnp.maximum, s)
        m_run = cm if m_run is None else jnp.maximum(m_run, cm)
    return jnp.broadcast_to(jnp.max(m_run, axis=-1, keepdims=True), m_run.shape)


def _weighted_values(s_scr, m_rep, v_ref, vc_ref):
    acc = None
    for ref, r0, rows, c0 in _key_chunks(v_ref, vc_ref):
        tiles = [jnp.exp2(s_scr[:, c0 + t * LANES:c0 + (t + 1) * LANES] - m_rep) for t in range(rows // LANES)]
        rhs = jnp.concatenate([ref[0, r0:r0 + rows, :], jnp.ones((rows, LANES), BF16)], axis=1)
        pv = _dot(jnp.concatenate(tiles, axis=1).astype(BF16), rhs)
        acc = pv if acc is None else acc + pv
    return acc


def _attn_kernel(q_ref, k_ref, kc_ref, vp_ref, vcp_ref, vn_ref, vcn_ref, o_ref,
                 s0_scr, s1_scr, m0_scr, m1_scr, o0_scr):
    t = pl.program_id(0)

    @pl.when(t == 0)
    def _():
        s1_scr[...] = jnp.zeros_like(s1_scr)
        m1_scr[...] = jnp.zeros_like(m1_scr)
        o0_scr[...] = jnp.ones_like(o0_scr)

    @pl.when(t >= 0)
    def _():
        m0_scr[...] = _scores(q_ref, k_ref, kc_ref, 0, s0_scr)
        acc1 = _weighted_values(s1_scr, m1_scr[...], vp_ref, vcp_ref)
        acc0 = o0_scr[...]
        o0 = acc0[:, :LANES] / acc0[:, LANES:]
        o1 = acc1[:, :LANES] / acc1[:, LANES:]
        lane = lax.broadcasted_iota(jnp.int32, o0.shape, 1)
        o_ref[0] = jnp.where(lane < V_HEAD_DIM, o0, o1).astype(BF16)

    @pl.when(t < pl.num_programs(0))
    def _():
        m1_scr[...] = _scores(q_ref, k_ref, kc_ref, 1, s1_scr)
        o0_scr[...] = _weighted_values(s0_scr, m0_scr[...], vn_ref, vcn_ref)


def _attention(q, k, v, kc, vc):
    b, s, _ = q.shape
    l = kc.shape[1]
    tq = _Plan.q_tile
    groups = N_HEADS // HEADS_PER_STEP
    n_q = s // tq
    tiles = b * groups * n_q
    pair_w = HEADS_PER_STEP * HEAD_SLOT
    o_w = HEADS_PER_STEP * V_HEAD_DIM

    def decode(u):
        return u // (groups * n_q), (u // n_q) % groups, u % n_q

    def cur(t):
        return decode(jnp.minimum(t, tiles - 1))

    def prev(t):
        return decode(jnp.maximum(t - 1, 0))

    def keys_of(which, rows, width):
        return pl.BlockSpec((1, rows, width), lambda t: (which(t)[0], 0, which(t)[1]))

    def rows_of(which, width):
        return pl.BlockSpec((1, tq, width), lambda t: (which(t)[0], which(t)[2], which(t)[1]))

    return pl.pallas_call(
        _attn_kernel,
        grid=(tiles + 1,),
        in_specs=[rows_of(cur, pair_w), keys_of(cur, s, pair_w), keys_of(cur, l, pair_w),
                  keys_of(prev, s, o_w), keys_of(prev, l, o_w), keys_of(cur, s, o_w), keys_of(cur, l, o_w)],
        out_specs=rows_of(prev, o_w),
        out_shape=jax.ShapeDtypeStruct((b, s, N_HEADS * V_HEAD_DIM), BF16),
        scratch_shapes=[pltpu.VMEM((tq, s + l), F32), pltpu.VMEM((tq, s + l), F32),
                        pltpu.VMEM((tq, LANES), F32), pltpu.VMEM((tq, LANES), F32),
                        pltpu.VMEM((tq, 2 * LANES), F32)],
        compiler_params=pltpu.CompilerParams(dimension_semantics=("arbitrary",),
                                             vmem_limit_bytes=_Plan.vmem_limit),
        name="attention",
    )(q, k, kc, v, vc, v, vc)


def _mlp_kernel(x_ref, a_ref, y_ref, mod_ref, woa_ref, woc_ref, w1_ref, w2_ref, gf_ref, o_ref):
    gate1 = mod_ref[0, 2:3, :]
    shift2, scale2, gate2 = mod_ref[0, 3:4, :], mod_ref[0, 4:5, :], mod_ref[0, 5:6, :]
    sub = x_ref.shape[1] // _Plan.mlp_sub_tiles
    rows = [slice(r * sub, (r + 1) * sub) for r in range(_Plan.mlp_sub_tiles)]
    mix = [_dot(a_ref[0, r, :], woa_ref[...]) + _dot(y_ref[0, r, :], woc_ref[...]) for r in rows]
    x1 = [x_ref[0, r, :] + gate1 * m for r, m in zip(rows, mix)]
    hmod = [(_rms(v) * (1.0 + scale2) + shift2).astype(BF16) for v in x1]
    fc = _Plan.ff_chunk
    acc = [None] * len(rows)
    for j in range(w1_ref.shape[1] // fc):
        u = [jnp.maximum(_dot(h, w1_ref[:, j * fc:(j + 1) * fc]), 0.0) for h in hmod]
        for r, v in enumerate(u):
            part = _dot((v * v).astype(BF16), w2_ref[j * fc:(j + 1) * fc, :])
            acc[r] = part if acc[r] is None else acc[r] + part
    for r, v, a in zip(rows, x1, acc):
        o_ref[0, r, :] = _rms(v + gate2 * a) * gf_ref[...]


def _out_proj_mlp(x, attn, conv, mod, wo_a, wo_c, w1, w2, gf):
    b, s, d = x.shape
    tm = _Plan.mlp_tile
    tok = lambda w: pl.BlockSpec((1, tm, w), lambda bi, i: (bi, i, 0))
    return pl.pallas_call(
        _mlp_kernel,
        grid=(b, s // tm),
        in_specs=[tok(d), tok(attn.shape[2]), tok(conv.shape[2]),
                  pl.BlockSpec((1,) + mod.shape[1:], lambda bi, i: (bi, 0, 0)),
                  _const_spec(wo_a.shape), _const_spec(wo_c.shape),
                  _const_spec(w1.shape), _const_spec(w2.shape), _const_spec(gf.shape)],
        out_specs=tok(d),
        out_shape=jax.ShapeDtypeStruct((b, s, d), F32),
        compiler_params=pltpu.CompilerParams(dimension_semantics=("arbitrary", "arbitrary"),
                                             vmem_limit_bytes=_Plan.vmem_limit),
        name="out_proj_mlp",
    )(x, attn, conv, mod, wo_a, wo_c, w1, w2, gf)


def _rope_lanes(w):
    half = QK_ROPE_DIM // 4
    run = lambda k: w[..., k * half:(k + 1) * half]
    x1, x2 = [run(0), run(2)], [run(1), run(3)]
    return jnp.concatenate(x1 + x2 + x1 + x2, axis=-1)


def _rope_tables(rows, scale):
    half = QK_ROPE_DIM // 4
    f32 = np.float32
    pos = np.arange(rows * GRID_W)
    freqs = f32(ROPE_THETA) ** (-np.arange(0, 2 * half, 2, dtype=f32) / f32(2 * half))
    ang = np.concatenate([(pos // GRID_W).astype(f32)[:, None] * freqs,
                          (pos % GRID_W).astype(f32)[:, None] * freqs], axis=1)
    cos, sin = np.cos(ang), np.sin(ang)
    zeros = lambda w: np.zeros((pos.shape[0], w), f32)
    pad = LANES - QK_NOPE_DIM - 2 * ROPE_HALF
    cos_k = np.concatenate([zeros(QK_NOPE_DIM), cos, cos, zeros(pad)], axis=1)
    sin_k = np.concatenate([zeros(QK_NOPE_DIM), -sin, sin, zeros(pad)], axis=1)
    cos_q = np.concatenate([np.ones((pos.shape[0], QK_NOPE_DIM), f32), cos, cos, zeros(pad)], axis=1)
    return (cos_q * f32(scale)).astype(f32), (sin_k * f32(scale)).astype(f32), cos_k, sin_k


def kernel(x, c, ctx, c_ctx, w_mod, b_mod, w_in, q_norm_g, w_uq, kv_norm_g, w_ukv,
           conv_w, w_out, w_mlp1, w_mlp2, final_norm_g):
    b, s, d = x.shape
    l = ctx.shape[1]
    depth = w_mod.shape[0]
    assert depth == 1, "single-layer configuration"
    assert s % _Plan.tok_tile == 0 and s % _Plan.q_tile == 0 and s % _Plan.kv_chunk == 0
    assert s % GRID_W == 0 and l % LANES == 0 and N_HEADS % HEADS_PER_STEP == 0
    cw = conv_w.shape[2]
    assert w_in.shape[2] == MLA_IN + 3 * cw and d == N_HEADS * V_HEAD_DIM + cw

    ctx_row = b
    rows = -(-(b + 1) // SUBLANES_F32) * SUBLANES_F32
    cvec = jnp.concatenate([c, c_ctx[None, :], jnp.zeros((rows - b - 1, d), F32)], axis=0)
    mod = _adaln(cvec, w_mod[0], b_mod[0][None, :]).reshape(rows, 6, d)

    wi = w_in[0]
    n_lat = Q_LORA_RANK + KV_LORA_RANK
    w_mla = jnp.concatenate([wi[:, :n_lat], jnp.zeros((d, LANES - 2 * QK_ROPE_DIM), F32),
                             _rope_lanes(wi[:, n_lat:MLA_IN])], axis=1).astype(BF16)
    w_conv = wi[:, MLA_IN:].astype(BF16)
    wq = w_uq[0].reshape(Q_LORA_RANK, N_HEADS, QK_DIM)
    w_uq_p = jnp.concatenate([wq[:, :, :QK_NOPE_DIM], _rope_lanes(wq[:, :, QK_NOPE_DIM:])], axis=2)
    w_uq_p = w_uq_p.reshape(Q_LORA_RANK, N_HEADS * HEAD_SLOT).astype(BF16)
    wkv = w_ukv[0].reshape(KV_LORA_RANK, N_HEADS, QK_NOPE_DIM + V_HEAD_DIM)
    wk = jnp.concatenate([wkv[:, :, :QK_NOPE_DIM],
                          jnp.zeros((KV_LORA_RANK, N_HEADS, HEAD_SLOT - QK_NOPE_DIM), F32)], axis=2)
    w_ukv_p = jnp.concatenate([wk.reshape(KV_LORA_RANK, N_HEADS * HEAD_SLOT),
                               wkv[:, :, QK_NOPE_DIM:].reshape(KV_LORA_RANK, N_HEADS * V_HEAD_DIM)],
                              axis=1).astype(BF16)

    cos_q, sin_q, cos_k, sin_k = _rope_tables(s // GRID_W, ATTN_SCALE * LOG2E)
    lane = np.arange(LANES)
    ctx_cos = np.broadcast_to(((lane >= QK_NOPE_DIM) & (lane < QK_DIM)).astype(np.float32), (l, LANES))
    ctx_sin = np.zeros((l, LANES), np.float32)

    q, k, v, conv = _project_tokens(x, mod, w_mla, w_conv, q_norm_g, w_uq_p, kv_norm_g, w_ukv_p, conv_w[0],
                                    (cos_q, sin_q, cos_k, sin_k))
    kc, vc = _project_ctx(ctx, mod, ctx_row, w_mla, kv_norm_g, w_ukv_p, (ctx_cos, ctx_sin))
    attn = _attention(q, k, v, kc, vc)

    nv = N_HEADS * V_HEAD_DIM
    wo = w_out[0].astype(BF16)
    return _out_proj_mlp(x, attn, conv, mod, wo[:nv], wo[nv:], w_mlp1[0].astype(BF16),
                         w_mlp2[0].astype(BF16), final_norm_g[None, :])
```

```python
import functools
import math

import jax
import jax.numpy as jnp
import numpy as np
from jax import lax
from jax.experimental import pallas as pl
from jax.experimental.pallas import tpu as pltpu

GRID_W = 64
N_HEADS = 8
QK_NOPE_DIM = 64
QK_ROPE_DIM = 32
V_HEAD_DIM = 64
Q_LORA_RANK = 256
KV_LORA_RANK = 128
CONV_K = 3
ROPE_THETA = 10000.0
EPS = 1e-6
QK_DIM = QK_NOPE_DIM + QK_ROPE_DIM
ROPE_HALF = QK_ROPE_DIM // 2
MLA_IN = Q_LORA_RANK + KV_LORA_RANK + QK_ROPE_DIM
ATTN_SCALE = 1.0 / math.sqrt(QK_DIM)
LOG2E = math.log2(math.e)

LANES = 128
SUBLANES_F32 = 8
BF16_ROWS = 16
VMEM_BYTES_V7X = 64 * 1024 * 1024

HEAD_SLOT = LANES
HEADS_PER_STEP = 2
HALO = SUBLANES_F32

F32 = jnp.float32
BF16 = jnp.bfloat16


class _Plan:
    tok_tile = 512
    mlp_tile = 1024
    mlp_sub_tiles = 2
    q_tile = 1024
    kv_chunk = 512
    ff_chunk = 1024
    mod_cols = 1536
    vmem_limit = 56 * 1024 * 1024
    assert vmem_limit < VMEM_BYTES_V7X


def _rms(x):
    return x * lax.rsqrt(jnp.mean(x * x, axis=-1, keepdims=True) + EPS)


def _dot(a, b):
    return jnp.dot(a, b, preferred_element_type=F32)


def _dot_nt(a, b):
    return lax.dot_general(a, b, (((1,), (1,)), ((), ())), preferred_element_type=F32)


def _rope(t, cos_tab, sin_tab):
    return t * cos_tab + pltpu.roll(t, LANES - ROPE_HALF, 1) * sin_tab


def _adaln_kernel(c_ref, w_ref, b_ref, o_ref):
    c = c_ref[...]
    a = (c / (1.0 + jnp.exp(-c))).astype(BF16)
    o_ref[...] = _dot(a, w_ref[...].astype(BF16)) + b_ref[...]


def _adaln(cvec, w_mod, b_mod):
    rows, d = cvec.shape
    n = w_mod.shape[1]
    tn = _Plan.mod_cols
    return pl.pallas_call(
        _adaln_kernel,
        grid=(n // tn,),
        in_specs=[pl.BlockSpec((rows, d), lambda j: (0, 0)),
                  pl.BlockSpec((d, tn), lambda j: (0, j)),
                  pl.BlockSpec((1, tn), lambda j: (0, j))],
        out_specs=pl.BlockSpec((rows, tn), lambda j: (0, j)),
        out_shape=jax.ShapeDtypeStruct((rows, n), F32),
        compiler_params=pltpu.CompilerParams(dimension_semantics=("arbitrary",),
                                             vmem_limit_bytes=_Plan.vmem_limit),
        name="adaln",
    )(cvec, w_mod, b_mod)


def _kv_from_z(z_ckv, z_rope, kvg_ref, wukv_ref, ck_ref, sk_ref, k_ref, v_ref):
    ckv = _rms(z_ckv) * kvg_ref[...]
    kv = _dot(ckv.astype(BF16), wukv_ref[...])
    k_rope = _rope(z_rope, ck_ref[...], sk_ref[...])
    for h in range(N_HEADS):
        sl = slice(h * HEAD_SLOT, (h + 1) * HEAD_SLOT)
        k_ref[0, :, sl] = (kv[:, sl] + k_rope).astype(BF16)
    v_ref[0] = kv[:, N_HEADS * HEAD_SLOT:].astype(BF16)


def _proj_kernel(n_cast, xp_ref, x_ref, xn_ref, mod_ref, wmla_ref, wconv_ref, qg_ref, wuq_ref, kvg_ref,
                 wukv_ref, cw_ref, cq_ref, sq_ref, ck_ref, sk_ref, *refs):
    cast_in, (q_ref, k_ref, v_ref, y_ref) = refs[:n_cast], refs[n_cast:n_cast + 4]
    cast_out, u_scr = refs[n_cast + 4:2 * n_cast + 4], refs[2 * n_cast + 4]
    i = pl.program_id(1)
    n = pl.num_programs(1)
    tm = x_ref.shape[1]
    cw = y_ref.shape[2]
    shift, scale = mod_ref[0, 0:1, :], mod_ref[0, 1:2, :]
    xe = jnp.concatenate([xp_ref[0], x_ref[0], xn_ref[0]], axis=0)
    hmod = (_rms(xe) * (1.0 + scale) + shift).astype(BF16)
    c0 = Q_LORA_RANK
    c1 = c0 + KV_LORA_RANK
    c2 = c1 + LANES
    main = slice(HALO, HALO + tm)
    z_mla = _dot(hmod, wmla_ref[...])[main]
    z_conv = _dot(hmod, wconv_ref[:, cw:])

    cq = _rms(z_mla[:, :c0]) * qg_ref[...]
    qf = _dot(cq.astype(BF16), wuq_ref[...])

    u = z_conv[:, :cw] * z_conv[:, cw:]
    row = lax.broadcasted_iota(jnp.int32, (tm + 2 * HALO, 1), 0)
    keep = ((row >= HALO) | (i > 0)) & ((row < HALO + tm) | (i < n - 1))
    u_scr[...] = jnp.where(keep, u, 0.0)
    w = cw_ref[...]
    y = (w[0:1] * u_scr[HALO - 1:HALO - 1 + tm]
         + w[1:2] * u_scr[HALO:HALO + tm]
         + w[2:3] * u_scr[HALO + 1:HALO + 1 + tm])

    cos_q, sin_q = cq_ref[...], sq_ref[...]
    for h in range(N_HEADS):
        sl = slice(h * HEAD_SLOT, (h + 1) * HEAD_SLOT)
        q_ref[0, :, sl] = _rope(qf[:, sl], cos_q, sin_q).astype(BF16)
    _kv_from_z(z_mla[:, c0:c1], z_mla[:, c1:c2], kvg_ref, wukv_ref, ck_ref, sk_ref, k_ref, v_ref)

    gate_b = _dot(hmod, wconv_ref[:, :cw])[main]
    y_ref[0] = (gate_b * y).astype(BF16)
    for src, dst in zip(cast_in, cast_out):
        dst[...] = src[...].astype(BF16)


def _ctx_kv_kernel(x_ref, mod_ref, win_ref, kvg_ref, wukv_ref, ck_ref, sk_ref, k_ref, v_ref):
    shift, scale = mod_ref[0, 0:1, :], mod_ref[0, 1:2, :]
    hmod = (_rms(x_ref[0]) * (1.0 + scale) + shift).astype(BF16)
    z = _dot(hmod, win_ref[...])
    _kv_from_z(z[:, :KV_LORA_RANK], z[:, KV_LORA_RANK:], kvg_ref, wukv_ref, ck_ref, sk_ref, k_ref, v_ref)


def _const_spec(shape):
    return pl.BlockSpec(shape, lambda *_: (0,) * len(shape))


def _cast_block_rows(rows, steps):
    per = -(-rows // steps)
    return -(-per // BF16_ROWS) * BF16_ROWS


def _project_tokens(x, mod, w_mla, w_conv, q_g, w_uq_p, kv_g, w_ukv_p, conv_w, tabs, weights_f32):
    b, s, d = x.shape
    tm = _Plan.tok_tile
    cw = conv_w.shape[1]
    nh = N_HEADS * HEAD_SLOT
    nv = N_HEADS * V_HEAD_DIM
    seq_tiles = s // tm
    steps = b * seq_tiles
    halo_blocks = tm // HALO
    last_halo = s // HALO - 1

    def cast_spec(w):
        rb = _cast_block_rows(w.shape[0], steps)
        assert w.shape[0] % rb == 0
        nblk = w.shape[0] // rb
        return pl.BlockSpec((rb, w.shape[1]),
                            lambda bi, i: (jnp.minimum((bi * seq_tiles + i) * nblk // steps, nblk - 1), 0))

    cast_specs = [cast_spec(w) for w in weights_f32]
    tab_spec = pl.BlockSpec((tm, LANES), lambda bi, i: (i, 0))
    tok = lambda w: pl.BlockSpec((1, tm, w), lambda bi, i: (bi, i, 0))
    return pl.pallas_call(
        functools.partial(_proj_kernel, len(weights_f32)),
        grid=(b, seq_tiles),
        in_specs=[
            pl.BlockSpec((1, HALO, d), lambda bi, i: (bi, jnp.maximum(i * halo_blocks - 1, 0), 0)),
            tok(d),
            pl.BlockSpec((1, HALO, d), lambda bi, i: (bi, jnp.minimum((i + 1) * halo_blocks, last_halo), 0)),
            pl.BlockSpec((1,) + mod.shape[1:], lambda bi, i: (bi, 0, 0)),
            _const_spec(w_mla.shape), _const_spec(w_conv.shape), _const_spec(q_g.shape), _const_spec(w_uq_p.shape),
            _const_spec(kv_g.shape), _const_spec(w_ukv_p.shape), _const_spec(conv_w.shape),
            tab_spec, tab_spec, tab_spec, tab_spec,
        ] + cast_specs,
        out_specs=[tok(nh), tok(nh), tok(nv), tok(cw)] + cast_specs,
        out_shape=[jax.ShapeDtypeStruct((b, s, nh), BF16), jax.ShapeDtypeStruct((b, s, nh), BF16),
                   jax.ShapeDtypeStruct((b, s, nv), BF16), jax.ShapeDtypeStruct((b, s, cw), BF16)]
                  + [jax.ShapeDtypeStruct(w.shape, BF16) for w in weights_f32],
        scratch_shapes=[pltpu.VMEM((tm + 2 * HALO, cw), F32)],
        compiler_params=pltpu.CompilerParams(dimension_semantics=("arbitrary", "arbitrary"),
                                             vmem_limit_bytes=_Plan.vmem_limit),
        name="token_proj",
    )(x, x, x, mod, w_mla, w_conv, q_g, w_uq_p, kv_g, w_ukv_p, conv_w, *tabs, *weights_f32)


def _project_ctx(ctx, mod, ctx_row, w_mla, kv_g, w_ukv_p, tabs):
    b, l, d = ctx.shape
    nh = N_HEADS * HEAD_SLOT
    nv = N_HEADS * V_HEAD_DIM
    kv_cols = 2 * LANES
    tok = lambda w: pl.BlockSpec((1, l, w), lambda bi: (bi, 0, 0))
    return pl.pallas_call(
        _ctx_kv_kernel,
        grid=(b,),
        in_specs=[
            tok(d),
            pl.BlockSpec((1,) + mod.shape[1:], lambda bi: (ctx_row, 0, 0)),
            pl.BlockSpec((d, kv_cols), lambda bi: (0, Q_LORA_RANK // kv_cols)),
            _const_spec(kv_g.shape), _const_spec(w_ukv_p.shape),
            _const_spec(tabs[0].shape), _const_spec(tabs[1].shape),
        ],
        out_specs=[tok(nh), tok(nv)],
        out_shape=[jax.ShapeDtypeStruct((b, l, nh), BF16), jax.ShapeDtypeStruct((b, l, nv), BF16)],
        compiler_params=pltpu.CompilerParams(dimension_semantics=("arbitrary",),
                                             vmem_limit_bytes=_Plan.vmem_limit),
        name="ctx_proj",
    )(ctx, mod, w_mla, kv_g, w_ukv_p, *tabs)


def _lane_tile_reduce(op, a):
    return functools.reduce(op, [a[:, t * LANES:(t + 1) * LANES] for t in range(a.shape[1] // LANES)])


def _key_chunks(main_ref, ctx_ref):
    s_len, tk = main_ref.shape[1], _Plan.kv_chunk
    return [(main_ref, c * tk, tk, c * tk) for c in range(s_len // tk)] + [(ctx_ref, 0, ctx_ref.shape[1], s_len)]


def _scores(q_ref, k_ref, kc_ref, head, s_scr):
    hs = slice(head * HEAD_SLOT, (head + 1) * HEAD_SLOT)
    q = q_ref[0, :, hs]
    m_run = None
    for ref, r0, rows, c0 in _key_chunks(k_ref, kc_ref):
        s = _dot_nt(q, ref[0, r0:r0 + rows, hs])
        s_scr[:, c0:c0 + rows] = s
        cm = _lane_tile_reduce(jnp.maximum, s)
        m_run = cm if m_run is None else jnp.maximum(m_run, cm)
    return jnp.broadcast_to(jnp.max(m_run, axis=-1, keepdims=True), m_run.shape)


def _weighted_values(s_scr, m_rep, v_ref, vc_ref):
    acc = None
    for ref, r0, rows, c0 in _key_chunks(v_ref, vc_ref):
        tiles = [jnp.exp2(s_scr[:, c0 + t * LANES:c0 + (t + 1) * LANES] - m_rep) for t in range(rows // LANES)]
        rhs = jnp.concatenate([ref[0, r0:r0 + rows, :], jnp.ones((rows, LANES), BF16)], axis=1)
        pv = _dot(jnp.concatenate(tiles, axis=1).astype(BF16), rhs)
        acc = pv if acc is None else acc + pv
    return acc


def _attn_kernel(q_ref, k_ref, kc_ref, vp_ref, vcp_ref, vn_ref, vcn_ref, o_ref,
                 s0_scr, s1_scr, m0_scr, m1_scr, o0_scr):
    t = pl.program_id(0)

    @pl.when(t == 0)
    def _():
        s1_scr[...] = jnp.zeros_like(s1_scr)
        m1_scr[...] = jnp.zeros_like(m1_scr)
        o0_scr[...] = jnp.ones_like(o0_scr)

    @pl.when(t >= 0)
    def _():
        m0_scr[...] = _scores(q_ref, k_ref, kc_ref, 0, s0_scr)
        acc1 = _weighted_values(s1_scr, m1_scr[...], vp_ref, vcp_ref)
        acc0 = o0_scr[...]
        o0 = acc0[:, :LANES] / acc0[:, LANES:]
        o1 = acc1[:, :LANES] / acc1[:, LANES:]
        lane = lax.broadcasted_iota(jnp.int32, o0.shape, 1)
        o_ref[0] = jnp.where(lane < V_HEAD_DIM, o0, o1).astype(BF16)

    @pl.when(t < pl.num_programs(0))
    def _():
        m1_scr[...] = _scores(q_ref, k_ref, kc_ref, 1, s1_scr)
        o0_scr[...] = _weighted_values(s0_scr, m0_scr[...], vn_ref, vcn_ref)


def _attention(q, k, v, kc, vc):
    b, s, _ = q.shape
    l = kc.shape[1]
    tq = _Plan.q_tile
    groups = N_HEADS // HEADS_PER_STEP
    n_q = s // tq
    tiles = b * groups * n_q
    pair_w = HEADS_PER_STEP * HEAD_SLOT
    o_w = HEADS_PER_STEP * V_HEAD_DIM

    def decode(u):
        return u // (groups * n_q), (u // n_q) % groups, u % n_q

    def cur(t):
        return decode(jnp.minimum(t, tiles - 1))

    def prev(t):
        return decode(jnp.maximum(t - 1, 0))

    def keys_of(which, rows, width):
        return pl.BlockSpec((1, rows, width), lambda t: (which(t)[0], 0, which(t)[1]))

    def rows_of(which, width):
        return pl.BlockSpec((1, tq, width), lambda t: (which(t)[0], which(t)[2], which(t)[1]))

    return pl.pallas_call(
        _attn_kernel,
        grid=(tiles + 1,),
        in_specs=[rows_of(cur, pair_w), keys_of(cur, s, pair_w), keys_of(cur, l, pair_w),
                  keys_of(prev, s, o_w), keys_of(prev, l, o_w), keys_of(cur, s, o_w), keys_of(cur, l, o_w)],
        out_specs=rows_of(prev, o_w),
        out_shape=jax.ShapeDtypeStruct((b, s, N_HEADS * V_HEAD_DIM), BF16),
        scratch_shapes=[pltpu.VMEM((tq, s + l), F32), pltpu.VMEM((tq, s + l), F32),
                        pltpu.VMEM((tq, LANES), F32), pltpu.VMEM((tq, LANES), F32),
                        pltpu.VMEM((tq, 2 * LANES), F32)],
        compiler_params=pltpu.CompilerParams(dimension_semantics=("arbitrary",),
                                             vmem_limit_bytes=_Plan.vmem_limit),
        name="attention",
    )(q, k, kc, v, vc, v, vc)


def _mlp_kernel(x_ref, a_ref, y_ref, mod_ref, wo_ref, w1_ref, w2_ref, gf_ref, o_ref):
    gate1 = mod_ref[0, 2:3, :]
    shift2, scale2, gate2 = mod_ref[0, 3:4, :], mod_ref[0, 4:5, :], mod_ref[0, 5:6, :]
    sub = x_ref.shape[1] // _Plan.mlp_sub_tiles
    rows = [slice(r * sub, (r + 1) * sub) for r in range(_Plan.mlp_sub_tiles)]
    nv = a_ref.shape[2]
    mix = [_dot(a_ref[0, r, :], wo_ref[:nv, :]) + _dot(y_ref[0, r, :], wo_ref[nv:, :]) for r in rows]
    x1 = [x_ref[0, r, :] + gate1 * m for r, m in zip(rows, mix)]
    hmod = [(_rms(v) * (1.0 + scale2) + shift2).astype(BF16) for v in x1]
    fc = _Plan.ff_chunk
    acc = [None] * len(rows)
    for j in range(w1_ref.shape[1] // fc):
        u = [jnp.maximum(_dot(h, w1_ref[:, j * fc:(j + 1) * fc]), 0.0) for h in hmod]
        for r, v in enumerate(u):
            part = _dot((v * v).astype(BF16), w2_ref[j * fc:(j + 1) * fc, :])
            acc[r] = part if acc[r] is None else acc[r] + part
    for r, v, a in zip(rows, x1, acc):
        o_ref[0, r, :] = _rms(v + gate2 * a) * gf_ref[...]


def _out_proj_mlp(x, attn, conv, mod, wo, w1, w2, gf):
    b, s, d = x.shape
    tm = _Plan.mlp_tile
    tok = lambda w: pl.BlockSpec((1, tm, w), lambda bi, i: (bi, i, 0))
    return pl.pallas_call(
        _mlp_kernel,
        grid=(b, s // tm),
        in_specs=[tok(d), tok(attn.shape[2]), tok(conv.shape[2]),
                  pl.BlockSpec((1,) + mod.shape[1:], lambda bi, i: (bi, 0, 0)),
                  _const_spec(wo.shape), _const_spec(w1.shape), _const_spec(w2.shape), _const_spec(gf.shape)],
        out_specs=tok(d),
        out_shape=jax.ShapeDtypeStruct((b, s, d), F32),
        compiler_params=pltpu.CompilerParams(dimension_semantics=("arbitrary", "arbitrary"),
                                             vmem_limit_bytes=_Plan.vmem_limit),
        name="out_proj_mlp",
    )(x, attn, conv, mod, wo, w1, w2, gf)


def _rope_lanes(w):
    half = QK_ROPE_DIM // 4
    run = lambda k: w[..., k * half:(k + 1) * half]
    x1, x2 = [run(0), run(2)], [run(1), run(3)]
    return jnp.concatenate(x1 + x2 + x1 + x2, axis=-1)


def _rope_tables(rows, scale):
    half = QK_ROPE_DIM // 4
    f32 = np.float32
    pos = np.arange(rows * GRID_W)
    freqs = f32(ROPE_THETA) ** (-np.arange(0, 2 * half, 2, dtype=f32) / f32(2 * half))
    ang = np.concatenate([(pos // GRID_W).astype(f32)[:, None] * freqs,
                          (pos % GRID_W).astype(f32)[:, None] * freqs], axis=1)
    cos, sin = np.cos(ang), np.sin(ang)
    zeros = lambda w: np.zeros((pos.shape[0], w), f32)
    pad = LANES - QK_NOPE_DIM - 2 * ROPE_HALF
    cos_k = np.concatenate([zeros(QK_NOPE_DIM), cos, cos, zeros(pad)], axis=1)
    sin_k = np.concatenate([zeros(QK_NOPE_DIM), -sin, sin, zeros(pad)], axis=1)
    cos_q = np.concatenate([np.ones((pos.shape[0], QK_NOPE_DIM), f32), cos, cos, zeros(pad)], axis=1)
    return (cos_q * f32(scale)).astype(f32), (sin_k * f32(scale)).astype(f32), cos_k, sin_k


def kernel(x, c, ctx, c_ctx, w_mod, b_mod, w_in, q_norm_g, w_uq, kv_norm_g, w_ukv,
           conv_w, w_out, w_mlp1, w_mlp2, final_norm_g):
    b, s, d = x.shape
    l = ctx.shape[1]
    depth = w_mod.shape[0]
    assert depth == 1, "single-layer configuration"
    assert s % _Plan.tok_tile == 0 and s % _Plan.q_tile == 0 and s % _Plan.kv_chunk == 0
    assert s % GRID_W == 0 and l % LANES == 0 and N_HEADS % HEADS_PER_STEP == 0
    cw = conv_w.shape[2]
    assert w_in.shape[2] == MLA_IN + 3 * cw and d == N_HEADS * V_HEAD_DIM + cw

    ctx_row = b
    rows = -(-(b + 1) // SUBLANES_F32) * SUBLANES_F32
    cvec = jnp.concatenate([c, c_ctx[None, :], jnp.zeros((rows - b - 1, d), F32)], axis=0)
    mod = _adaln(cvec, w_mod[0], b_mod[0][None, :]).reshape(rows, 6, d)

    wi = w_in[0]
    n_lat = Q_LORA_RANK + KV_LORA_RANK
    w_mla = jnp.concatenate([wi[:, :n_lat], jnp.zeros((d, LANES - 2 * QK_ROPE_DIM), F32),
                             _rope_lanes(wi[:, n_lat:MLA_IN])], axis=1).astype(BF16)
    w_conv = wi[:, MLA_IN:].astype(BF16)
    wq = w_uq[0].reshape(Q_LORA_RANK, N_HEADS, QK_DIM)
    w_uq_p = jnp.concatenate([wq[:, :, :QK_NOPE_DIM], _rope_lanes(wq[:, :, QK_NOPE_DIM:])], axis=2)
    w_uq_p = w_uq_p.reshape(Q_LORA_RANK, N_HEADS * HEAD_SLOT).astype(BF16)
    wkv = w_ukv[0].reshape(KV_LORA_RANK, N_HEADS, QK_NOPE_DIM + V_HEAD_DIM)
    wk = jnp.concatenate([wkv[:, :, :QK_NOPE_DIM],
                          jnp.zeros((KV_LORA_RANK, N_HEADS, HEAD_SLOT - QK_NOPE_DIM), F32)], axis=2)
    w_ukv_p = jnp.concatenate([wk.reshape(KV_LORA_RANK, N_HEADS * HEAD_SLOT),
                               wkv[:, :, QK_NOPE_DIM:].reshape(KV_LORA_RANK, N_HEADS * V_HEAD_DIM)],
                              axis=1).astype(BF16)

    cos_q, sin_q, cos_k, sin_k = _rope_tables(s // GRID_W, ATTN_SCALE * LOG2E)
    lane = np.arange(LANES)
    ctx_cos = np.broadcast_to(((lane >= QK_NOPE_DIM) & (lane < QK_DIM)).astype(np.float32), (l, LANES))
    ctx_sin = np.zeros((l, LANES), np.float32)

    q, k, v, conv, w1, w2 = _project_tokens(x, mod, w_mla, w_conv, q_norm_g, w_uq_p, kv_norm_g, w_ukv_p,
                                            conv_w[0], (cos_q, sin_q, cos_k, sin_k), (w_mlp1[0], w_mlp2[0]))
    kc, vc = _project_ctx(ctx, mod, ctx_row, w_mla, kv_norm_g, w_ukv_p, (ctx_cos, ctx_sin))
    attn = _attention(q, k, v, kc, vc)

    return _out_proj_mlp(x, attn, conv, mod, w_out[0].astype(BF16), w1, w2, final_norm_g[None, :])
```

```python
import functools
import math

import jax
import jax.numpy as jnp
import numpy as np
from jax import lax
from jax.experimental import pallas as pl
from jax.experimental.pallas import tpu as pltpu

GRID_W = 64
N_HEADS = 8
QK_NOPE_DIM = 64
QK_ROPE_DIM = 32
V_HEAD_DIM = 64
Q_LORA_RANK = 256
KV_LORA_RANK = 128
CONV_K = 3
ROPE_THETA = 10000.0
EPS = 1e-6
QK_DIM = QK_NOPE_DIM + QK_ROPE_DIM
ROPE_HALF = QK_ROPE_DIM // 2
MLA_IN = Q_LORA_RANK + KV_LORA_RANK + QK_ROPE_DIM
ATTN_SCALE = 1.0 / math.sqrt(QK_DIM)
LOG2E = math.log2(math.e)

LANES = 128
SUBLANES_F32 = 8
BF16_ROWS = 16
VMEM_BYTES_V7X = 64 * 1024 * 1024

HEAD_SLOT = LANES
HEADS_PER_STEP = 2
HALO = SUBLANES_F32

F32 = jnp.float32
BF16 = jnp.bfloat16


class _Plan:
    tok_tile = 1024
    tok_sub_tiles = 2
    tok_issue_order = (0, 0, 1, 0, 1, 0, 1, 1)
    ctx_batches_per_step = 4
    mlp_tile = 1024
    mlp_sub_tiles = 2
    q_tile = 1024
    kv_chunk = 512
    ff_chunk = 1024
    mod_cols = 1536
    vmem_limit = 56 * 1024 * 1024
    assert vmem_limit < VMEM_BYTES_V7X


def _rms(x):
    return x * lax.rsqrt(jnp.mean(x * x, axis=-1, keepdims=True) + EPS)


def _dot(a, b):
    return jnp.dot(a, b, preferred_element_type=F32)


def _dot_nt(a, b):
    return lax.dot_general(a, b, (((1,), (1,)), ((), ())), preferred_element_type=F32)


def _rope(t, cos_tab, sin_tab):
    return t * cos_tab + pltpu.roll(t, LANES - ROPE_HALF, 1) * sin_tab


def _adaln_kernel(c_ref, w_ref, b_ref, o_ref):
    c = c_ref[...]
    a = (c / (1.0 + jnp.exp(-c))).astype(BF16)
    o_ref[...] = _dot(a, w_ref[...].astype(BF16)) + b_ref[...]


def _adaln(cvec, w_mod, b_mod):
    rows, d = cvec.shape
    n = w_mod.shape[1]
    tn = _Plan.mod_cols
    return pl.pallas_call(
        _adaln_kernel,
        grid=(n // tn,),
        in_specs=[pl.BlockSpec((rows, d), lambda j: (0, 0)),
                  pl.BlockSpec((d, tn), lambda j: (0, j)),
                  pl.BlockSpec((1, tn), lambda j: (0, j))],
        out_specs=pl.BlockSpec((rows, tn), lambda j: (0, j)),
        out_shape=jax.ShapeDtypeStruct((rows, n), F32),
        compiler_params=pltpu.CompilerParams(dimension_semantics=("arbitrary",),
                                             vmem_limit_bytes=_Plan.vmem_limit),
        name="adaln",
    )(cvec, w_mod, b_mod)


def _kv_from_z(z_ckv, z_rope, kvg_ref, wukv_ref, cos_k, sin_k, k_ref, v_ref, rows):
    ckv = _rms(z_ckv) * kvg_ref[...]
    kv = _dot(ckv.astype(BF16), wukv_ref[...])
    k_rope = _rope(z_rope, cos_k, sin_k)
    for h in range(N_HEADS):
        sl = slice(h * HEAD_SLOT, (h + 1) * HEAD_SLOT)
        k_ref[0, rows, sl] = (kv[:, sl] + k_rope).astype(BF16)
    v_ref[0, rows, :] = kv[:, N_HEADS * HEAD_SLOT:].astype(BF16)


def _proj_sub_tile(j, n_sub, xp_ref, x_ref, xn_ref, mod_ref, wmla_ref, wconv_ref, qg_ref, wuq_ref, kvg_ref,
                   wukv_ref, cw_ref, cq_ref, sq_ref, ck_ref, sk_ref, q_ref, k_ref, v_ref, y_ref, u_scr):
    i = pl.program_id(1)
    n = pl.num_programs(1)
    sub = x_ref.shape[1] // n_sub
    cw = y_ref.shape[2]
    rows = slice(j * sub, (j + 1) * sub)
    main = slice(HALO, HALO + sub)
    c0 = Q_LORA_RANK
    c1 = c0 + KV_LORA_RANK
    c2 = c1 + LANES
    shift, scale = mod_ref[0, 0:1, :], mod_ref[0, 1:2, :]
    before = xp_ref[0] if j == 0 else x_ref[0, j * sub - HALO:j * sub, :]
    after = xn_ref[0] if j == n_sub - 1 else x_ref[0, (j + 1) * sub:(j + 1) * sub + HALO, :]
    xe = jnp.concatenate([before, x_ref[0, rows, :], after], axis=0)
    hmod = (_rms(xe) * (1.0 + scale) + shift).astype(BF16)
    z_mla = _dot(hmod, wmla_ref[...])[main]
    yield
    z_conv = _dot(hmod, wconv_ref[:, cw:])
    yield
    cq = _rms(z_mla[:, :c0]) * qg_ref[...]
    qf = _dot(cq.astype(BF16), wuq_ref[...])

    u = z_conv[:, :cw] * z_conv[:, cw:]
    row = lax.broadcasted_iota(jnp.int32, (sub + 2 * HALO, 1), 0)
    if j == 0:
        u = jnp.where((row >= HALO) | (i > 0), u, 0.0)
    if j == n_sub - 1:
        u = jnp.where((row < HALO + sub) | (i < n - 1), u, 0.0)
    u_scr[j] = u
    w = cw_ref[...]
    y = (w[0:1] * u_scr[j, HALO - 1:HALO - 1 + sub]
         + w[1:2] * u_scr[j, HALO:HALO + sub]
         + w[2:3] * u_scr[j, HALO + 1:HALO + 1 + sub])

    cos_q, sin_q = cq_ref[rows, :], sq_ref[rows, :]
    for h in range(N_HEADS):
        sl = slice(h * HEAD_SLOT, (h + 1) * HEAD_SLOT)
        q_ref[0, rows, sl] = _rope(qf[:, sl], cos_q, sin_q).astype(BF16)
    _kv_from_z(z_mla[:, c0:c1], z_mla[:, c1:c2], kvg_ref, wukv_ref, ck_ref[rows, :], sk_ref[rows, :],
               k_ref, v_ref, rows)
    yield
    gate_b = _dot(hmod, wconv_ref[:, :cw])[main]
    y_ref[0, rows, :] = (gate_b * y).astype(BF16)
    yield


def _proj_kernel(n_cast, *refs):
    n_in = 15
    ins, cast_in = refs[:n_in], refs[n_in:n_in + n_cast]
    outs = refs[n_in + n_cast:n_in + n_cast + 4]
    cast_out, u_scr = refs[n_in + n_cast + 4:n_in + 2 * n_cast + 4], refs[n_in + 2 * n_cast + 4]
    n_sub = _Plan.tok_sub_tiles
    tiles = [_proj_sub_tile(j, n_sub, *ins, *outs, u_scr) for j in range(n_sub)]
    for j in _Plan.tok_issue_order:
        next(tiles[j])
    for src, dst in zip(cast_in, cast_out):
        dst[...] = src[...].astype(BF16)


def _ctx_kv_kernel(x_ref, mod_ref, win_ref, kvg_ref, wukv_ref, ck_ref, sk_ref, k_ref, v_ref):
    shift, scale = mod_ref[0, 0:1, :], mod_ref[0, 1:2, :]
    hmod = (_rms(x_ref[0]) * (1.0 + scale) + shift).astype(BF16)
    z = _dot(hmod, win_ref[...])
    _kv_from_z(z[:, :KV_LORA_RANK], z[:, KV_LORA_RANK:], kvg_ref, wukv_ref, ck_ref[...], sk_ref[...],
               k_ref, v_ref, slice(None))


def _const_spec(shape):
    return pl.BlockSpec(shape, lambda *_: (0,) * len(shape))


def _cast_block_rows(rows, steps):
    per = -(-rows // steps)
    return -(-per // BF16_ROWS) * BF16_ROWS


def _project_tokens(x, mod, w_mla, w_conv, q_g, w_uq_p, kv_g, w_ukv_p, conv_w, tabs, weights_f32):
    b, s, d = x.shape
    tm = _Plan.tok_tile
    cw = conv_w.shape[1]
    nh = N_HEADS * HEAD_SLOT
    nv = N_HEADS * V_HEAD_DIM
    seq_tiles = s // tm
    steps = b * seq_tiles
    halo_blocks = tm // HALO
    last_halo = s // HALO - 1

    def cast_spec(w):
        rb = _cast_block_rows(w.shape[0], steps)
        assert w.shape[0] % rb == 0
        nblk = w.shape[0] // rb
        return pl.BlockSpec((rb, w.shape[1]),
                            lambda bi, i: (jnp.minimum((bi * seq_tiles + i) * nblk // steps, nblk - 1), 0))

    cast_specs = [cast_spec(w) for w in weights_f32]
    tab_spec = pl.BlockSpec((tm, LANES), lambda bi, i: (i, 0))
    tok = lambda w: pl.BlockSpec((1, tm, w), lambda bi, i: (bi, i, 0))
    return pl.pallas_call(
        functools.partial(_proj_kernel, len(weights_f32)),
        grid=(b, seq_tiles),
        in_specs=[
            pl.BlockSpec((1, HALO, d), lambda bi, i: (bi, jnp.maximum(i * halo_blocks - 1, 0), 0)),
            tok(d),
            pl.BlockSpec((1, HALO, d), lambda bi, i: (bi, jnp.minimum((i + 1) * halo_blocks, last_halo), 0)),
            pl.BlockSpec((1,) + mod.shape[1:], lambda bi, i: (bi, 0, 0)),
            _const_spec(w_mla.shape), _const_spec(w_conv.shape), _const_spec(q_g.shape), _const_spec(w_uq_p.shape),
            _const_spec(kv_g.shape), _const_spec(w_ukv_p.shape), _const_spec(conv_w.shape),
            tab_spec, tab_spec, tab_spec, tab_spec,
        ] + cast_specs,
        out_specs=[tok(nh), tok(nh), tok(nv), tok(cw)] + cast_specs,
        out_shape=[jax.ShapeDtypeStruct((b, s, nh), BF16), jax.ShapeDtypeStruct((b, s, nh), BF16),
                   jax.ShapeDtypeStruct((b, s, nv), BF16), jax.ShapeDtypeStruct((b, s, cw), BF16)]
                  + [jax.ShapeDtypeStruct(w.shape, BF16) for w in weights_f32],
        scratch_shapes=[pltpu.VMEM((_Plan.tok_sub_tiles, tm // _Plan.tok_sub_tiles + 2 * HALO, cw), F32)],
        compiler_params=pltpu.CompilerParams(dimension_semantics=("arbitrary", "arbitrary"),
                                             vmem_limit_bytes=_Plan.vmem_limit),
        name="token_proj",
    )(x, x, x, mod, w_mla, w_conv, q_g, w_uq_p, kv_g, w_ukv_p, conv_w, *tabs, *weights_f32)


def _project_ctx(ctx, mod, ctx_row, w_mla, kv_g, w_ukv_p, tabs):
    b, l, d = ctx.shape
    nh = N_HEADS * HEAD_SLOT
    nv = N_HEADS * V_HEAD_DIM
    kv_cols = 2 * LANES
    group = _Plan.ctx_batches_per_step
    assert b % group == 0
    rows = group * l
    tabs = [np.tile(t, (group, 1)) for t in tabs]
    tok = lambda w: pl.BlockSpec((1, rows, w), lambda bi: (bi, 0, 0))
    kc, vc = pl.pallas_call(
        _ctx_kv_kernel,
        grid=(b // group,),
        in_specs=[
            tok(d),
            pl.BlockSpec((1,) + mod.shape[1:], lambda bi: (ctx_row, 0, 0)),
            pl.BlockSpec((d, kv_cols), lambda bi: (0, Q_LORA_RANK // kv_cols)),
            _const_spec(kv_g.shape), _const_spec(w_ukv_p.shape),
            _const_spec(tabs[0].shape), _const_spec(tabs[1].shape),
        ],
        out_specs=[tok(nh), tok(nv)],
        out_shape=[jax.ShapeDtypeStruct((b // group, rows, nh), BF16),
                   jax.ShapeDtypeStruct((b // group, rows, nv), BF16)],
        compiler_params=pltpu.CompilerParams(dimension_semantics=("arbitrary",),
                                             vmem_limit_bytes=_Plan.vmem_limit),
        name="ctx_proj",
    )(ctx.reshape(b // group, rows, d), mod, w_mla, kv_g, w_ukv_p, *tabs)
    return kc.reshape(b, l, nh), vc.reshape(b, l, nv)


def _lane_tile_reduce(op, a):
    return functools.reduce(op, [a[:, t * LANES:(t + 1) * LANES] for t in range(a.shape[1] // LANES)])


def _key_chunks(main_ref, ctx_ref):
    s_len, tk = main_ref.shape[1], _Plan.kv_chunk
    return [(main_ref, c * tk, tk, c * tk) for c in range(s_len // tk)] + [(ctx_ref, 0, ctx_ref.shape[1], s_len)]


def _scores(q_ref, k_ref, kc_ref, head, s_scr):
    hs = slice(head * HEAD_SLOT, (head + 1) * HEAD_SLOT)
    q = q_ref[0, :, hs]
    m_run = None
    for ref, r0, rows, c0 in _key_chunks(k_ref, kc_ref):
        s = _dot_nt(q, ref[0, r0:r0 + rows, hs])
        s_scr[:, c0:c0 + rows] = s
        cm = _lane_tile_reduce(jnp.maximum, s)
        m_run = cm if m_run is None else jnp.maximum(m_run, cm)
    return jnp.broadcast_to(jnp.max(m_run, axis=-1, keepdims=True), m_run.shape)


def _weighted_values(s_scr, m_rep, v_ref, vc_ref):
    acc = None
    for ref, r0, rows, c0 in _key_chunks(v_ref, vc_ref):
        tiles = [jnp.exp2(s_scr[:, c0 + t * LANES:c0 + (t + 1) * LANES] - m_rep) for t in range(rows // LANES)]
        rhs = jnp.concatenate([ref[0, r0:r0 + rows, :], jnp.ones((rows, LANES), BF16)], axis=1)
        pv = _dot(jnp.concatenate(tiles, axis=1).astype(BF16), rhs)
        acc = pv if acc is None else acc + pv
    return acc


def _attn_kernel(q_ref, k_ref, kc_ref, vp_ref, vcp_ref, vn_ref, vcn_ref, o_ref,
                 s0_scr, s1_scr, m0_scr, m1_scr, o0_scr):
    t = pl.program_id(0)

    @pl.when(t == 0)
    def _():
        s1_scr[...] = jnp.zeros_like(s1_scr)
        m1_scr[...] = jnp.zeros_like(m1_scr)
        o0_scr[...] = jnp.ones_like(o0_scr)

    @pl.when(t >= 0)
    def _():
        m0_scr[...] = _scores(q_ref, k_ref, kc_ref, 0, s0_scr)
        acc1 = _weighted_values(s1_scr, m1_scr[...], vp_ref, vcp_ref)
        acc0 = o0_scr[...]
        o0 = acc0[:, :LANES] / acc0[:, LANES:]
        o1 = acc1[:, :LANES] / acc1[:, LANES:]
        lane = lax.broadcasted_iota(jnp.int32, o0.shape, 1)
        o_ref[0] = jnp.where(lane < V_HEAD_DIM, o0, o1).astype(BF16)

    @pl.when(t < pl.num_programs(0))
    def _():
        m1_scr[...] = _scores(q_ref, k_ref, kc_ref, 1, s1_scr)
        o0_scr[...] = _weighted_values(s0_scr, m0_scr[...], vn_ref, vcn_ref)


def _attention(q, k, v, kc, vc):
    b, s, _ = q.shape
    l = kc.shape[1]
    tq = _Plan.q_tile
    groups = N_HEADS // HEADS_PER_STEP
    n_q = s // tq
    tiles = b * groups * n_q
    pair_w = HEADS_PER_STEP * HEAD_SLOT
    o_w = HEADS_PER_STEP * V_HEAD_DIM

    def decode(u):
        return u // (groups * n_q), (u // n_q) % groups, u % n_q

    def cur(t):
        return decode(jnp.minimum(t, tiles - 1))

    def prev(t):
        return decode(jnp.maximum(t - 1, 0))

    def keys_of(which, rows, width):
        return pl.BlockSpec((1, rows, width), lambda t: (which(t)[0], 0, which(t)[1]))

    def rows_of(which, width):
        return pl.BlockSpec((1, tq, width), lambda t: (which(t)[0], which(t)[2], which(t)[1]))

    return pl.pallas_call(
        _attn_kernel,
        grid=(tiles + 1,),
        in_specs=[rows_of(cur, pair_w), keys_of(cur, s, pair_w), keys_of(cur, l, pair_w),
                  keys_of(prev, s, o_w), keys_of(prev, l, o_w), keys_of(cur, s, o_w), keys_of(cur, l, o_w)],
        out_specs=rows_of(prev, o_w),
        out_shape=jax.ShapeDtypeStruct((b, s, N_HEADS * V_HEAD_DIM), BF16),
        scratch_shapes=[pltpu.VMEM((tq, s + l), F32), pltpu.VMEM((tq, s + l), F32),
                        pltpu.VMEM((tq, LANES), F32), pltpu.VMEM((tq, LANES), F32),
                        pltpu.VMEM((tq, 2 * LANES), F32)],
        compiler_params=pltpu.CompilerParams(dimension_semantics=("arbitrary",),
                                             vmem_limit_bytes=_Plan.vmem_limit),
        name="attention",
    )(q, k, kc, v, vc, v, vc)


def _mlp_kernel(x_ref, a_ref, y_ref, mod_ref, wo_ref, w1_ref, w2_ref, gf_ref, o_ref):
    gate1 = mod_ref[0, 2:3, :]
    shift2, scale2, gate2 = mod_ref[0, 3:4, :], mod_ref[0, 4:5, :], mod_ref[0, 5:6, :]
    sub = x_ref.shape[1] // _Plan.mlp_sub_tiles
    rows = [slice(r * sub, (r + 1) * sub) for r in range(_Plan.mlp_sub_tiles)]
    nv = a_ref.shape[2]
    mix = [_dot(a_ref[0, r, :], wo_ref[:nv, :]) + _dot(y_ref[0, r, :], wo_ref[nv:, :]) for r in rows]
    x1 = [x_ref[0, r, :] + gate1 * m for r, m in zip(rows, mix)]
    hmod = [(_rms(v) * (1.0 + scale2) + shift2).astype(BF16) for v in x1]
    fc = _Plan.ff_chunk
    acc = [None] * len(rows)
    for j in range(w1_ref.shape[1] // fc):
        u = [jnp.maximum(_dot(h, w1_ref[:, j * fc:(j + 1) * fc]), 0.0) for h in hmod]
        for r, v in enumerate(u):
            part = _dot((v * v).astype(BF16), w2_ref[j * fc:(j + 1) * fc, :])
            acc[r] = part if acc[r] is None else acc[r] + part
    for r, v, a in zip(rows, x1, acc):
        o_ref[0, r, :] = _rms(v + gate2 * a) * gf_ref[...]


def _out_proj_mlp(x, attn, conv, mod, wo, w1, w2, gf):
    b, s, d = x.shape
    tm = _Plan.mlp_tile
    tok = lambda w: pl.BlockSpec((1, tm, w), lambda bi, i: (bi, i, 0))
    return pl.pallas_call(
        _mlp_kernel,
        grid=(b, s // tm),
        in_specs=[tok(d), tok(attn.shape[2]), tok(conv.shape[2]),
                  pl.BlockSpec((1,) + mod.shape[1:], lambda bi, i: (bi, 0, 0)),
                  _const_spec(wo.shape), _const_spec(w1.shape), _const_spec(w2.shape), _const_spec(gf.shape)],
        out_specs=tok(d),
        out_shape=jax.ShapeDtypeStruct((b, s, d), F32),
        compiler_params=pltpu.CompilerParams(dimension_semantics=("arbitrary", "arbitrary"),
                                             vmem_limit_bytes=_Plan.vmem_limit),
        name="out_proj_mlp",
    )(x, attn, conv, mod, wo, w1, w2, gf)


def _rope_lanes(w):
    half = QK_ROPE_DIM // 4
    run = lambda k: w[..., k * half:(k + 1) * half]
    x1, x2 = [run(0), run(2)], [run(1), run(3)]
    return jnp.concatenate(x1 + x2 + x1 + x2, axis=-1)


def _rope_tables(rows, scale):
    half = QK_ROPE_DIM // 4
    f32 = np.float32
    pos = np.arange(rows * GRID_W)
    freqs = f32(ROPE_THETA) ** (-np.arange(0, 2 * half, 2, dtype=f32) / f32(2 * half))
    ang = np.concatenate([(pos // GRID_W).astype(f32)[:, None] * freqs,
                          (pos % GRID_W).astype(f32)[:, None] * freqs], axis=1)
    cos, sin = np.cos(ang), np.sin(ang)
    zeros = lambda w: np.zeros((pos.shape[0], w), f32)
    pad = LANES - QK_NOPE_DIM - 2 * ROPE_HALF
    cos_k = np.concatenate([zeros(QK_NOPE_DIM), cos, cos, zeros(pad)], axis=1)
    sin_k = np.concatenate([zeros(QK_NOPE_DIM), -sin, sin, zeros(pad)], axis=1)
    cos_q = np.concatenate([np.ones((pos.shape[0], QK_NOPE_DIM), f32), cos, cos, zeros(pad)], axis=1)
    return (cos_q * f32(scale)).astype(f32), (sin_k * f32(scale)).astype(f32), cos_k, sin_k


def kernel(x, c, ctx, c_ctx, w_mod, b_mod, w_in, q_norm_g, w_uq, kv_norm_g, w_ukv,
           conv_w, w_out, w_mlp1, w_mlp2, final_norm_g):
    b, s, d = x.shape
    l = ctx.shape[1]
    depth = w_mod.shape[0]
    assert depth == 1, "single-layer configuration"
    assert s % _Plan.tok_tile == 0 and s % _Plan.q_tile == 0 and s % _Plan.kv_chunk == 0
    assert s % GRID_W == 0 and l % LANES == 0 and N_HEADS % HEADS_PER_STEP == 0
    cw = conv_w.shape[2]
    assert w_in.shape[2] == MLA_IN + 3 * cw and d == N_HEADS * V_HEAD_DIM + cw

    ctx_row = b
    rows = -(-(b + 1) // SUBLANES_F32) * SUBLANES_F32
    cvec = jnp.concatenate([c, c_ctx[None, :], jnp.zeros((rows - b - 1, d), F32)], axis=0)
    mod = _adaln(cvec, w_mod[0], b_mod[0][None, :]).reshape(rows, 6, d)

    wi = w_in[0]
    n_lat = Q_LORA_RANK + KV_LORA_RANK
    w_mla = jnp.concatenate([wi[:, :n_lat], jnp.zeros((d, LANES - 2 * QK_ROPE_DIM), F32),
                             _rope_lanes(wi[:, n_lat:MLA_IN])], axis=1).astype(BF16)
    w_conv = wi[:, MLA_IN:].astype(BF16)
    wq = w_uq[0].reshape(Q_LORA_RANK, N_HEADS, QK_DIM)
    w_uq_p = jnp.concatenate([wq[:, :, :QK_NOPE_DIM], _rope_lanes(wq[:, :, QK_NOPE_DIM:])], axis=2)
    w_uq_p = w_uq_p.reshape(Q_LORA_RANK, N_HEADS * HEAD_SLOT).astype(BF16)
    wkv = w_ukv[0].reshape(KV_LORA_RANK, N_HEADS, QK_NOPE_DIM + V_HEAD_DIM)
    wk = jnp.concatenate([wkv[:, :, :QK_NOPE_DIM],
                          jnp.zeros((KV_LORA_RANK, N_HEADS, HEAD_SLOT - QK_NOPE_DIM), F32)], axis=2)
    w_ukv_p = jnp.concatenate([wk.reshape(KV_LORA_RANK, N_HEADS * HEAD_SLOT),
                               wkv[:, :, QK_NOPE_DIM:].reshape(KV_LORA_RANK, N_HEADS * V_HEAD_DIM)],
                              axis=1).astype(BF16)

    cos_q, sin_q, cos_k, sin_k = _rope_tables(s // GRID_W, ATTN_SCALE * LOG2E)
    lane = np.arange(LANES)
    ctx_cos = np.broadcast_to(((lane >= QK_NOPE_DIM) & (lane < QK_DIM)).astype(np.float32), (l, LANES))
    ctx_sin = np.zeros((l, LANES), np.float32)

    q, k, v, conv, w1, w2 = _project_tokens(x, mod, w_mla, w_conv, q_norm_g, w_uq_p, kv_norm_g, w_ukv_p,
                                            conv_w[0], (cos_q, sin_q, cos_k, sin_k), (w_mlp1[0], w_mlp2[0]))
    kc, vc = _project_ctx(ctx, mod, ctx_row, w_mla, kv_norm_g, w_ukv_p, (ctx_cos, ctx_sin))
    attn = _attention(q, k, v, kc, vc)

    return _out_proj_mlp(x, attn, conv, mod, w_out[0].astype(BF16), w1, w2, final_norm_g[None, :])
```

```python
import functools
import math

import jax
import jax.numpy as jnp
import numpy as np
from jax import lax
from jax.experimental import pallas as pl
from jax.experimental.pallas import tpu as pltpu

GRID_W = 64
N_HEADS = 8
QK_NOPE_DIM = 64
QK_ROPE_DIM = 32
V_HEAD_DIM = 64
Q_LORA_RANK = 256
KV_LORA_RANK = 128
CONV_K = 3
ROPE_THETA = 10000.0
EPS = 1e-6
QK_DIM = QK_NOPE_DIM + QK_ROPE_DIM
ROPE_HALF = QK_ROPE_DIM // 2
MLA_IN = Q_LORA_RANK + KV_LORA_RANK + QK_ROPE_DIM
ATTN_SCALE = 1.0 / math.sqrt(QK_DIM)
LOG2E = math.log2(math.e)

LANES = 128
SUBLANES_F32 = 8
BF16_ROWS = 16
VMEM_BYTES_V7X = 64 * 1024 * 1024

HEAD_SLOT = LANES
HEADS_PER_STEP = 2
HALO = SUBLANES_F32

F32 = jnp.float32
BF16 = jnp.bfloat16


class _Plan:
    tok_tile = 1024
    tok_sub_tiles = 2
    tok_issue_order = (0, 0, 1, 0, 1, 0, 1, 1)
    ctx_batches_per_step = 4
    mlp_tile = 1024
    mlp_sub_tiles = 4
    q_tile = 1024
    kv_chunk = 512
    ff_chunk = 1024
    mod_cols = 1536
    vmem_limit = 56 * 1024 * 1024
    assert vmem_limit < VMEM_BYTES_V7X


def _rms(x):
    return x * lax.rsqrt(jnp.mean(x * x, axis=-1, keepdims=True) + EPS)


def _dot(a, b):
    return jnp.dot(a, b, preferred_element_type=F32)


def _dot_nt(a, b):
    return lax.dot_general(a, b, (((1,), (1,)), ((), ())), preferred_element_type=F32)


def _rope(t, cos_tab, sin_tab):
    return t * cos_tab + pltpu.roll(t, LANES - ROPE_HALF, 1) * sin_tab


def _adaln_kernel(c_ref, w_ref, b_ref, o_ref):
    c = c_ref[...]
    a = (c / (1.0 + jnp.exp(-c))).astype(BF16)
    o_ref[...] = _dot(a, w_ref[...].astype(BF16)) + b_ref[...]


def _adaln(cvec, w_mod, b_mod):
    rows, d = cvec.shape
    n = w_mod.shape[1]
    tn = _Plan.mod_cols
    return pl.pallas_call(
        _adaln_kernel,
        grid=(n // tn,),
        in_specs=[pl.BlockSpec((rows, d), lambda j: (0, 0)),
                  pl.BlockSpec((d, tn), lambda j: (0, j)),
                  pl.BlockSpec((1, tn), lambda j: (0, j))],
        out_specs=pl.BlockSpec((rows, tn), lambda j: (0, j)),
        out_shape=jax.ShapeDtypeStruct((rows, n), F32),
        compiler_params=pltpu.CompilerParams(dimension_semantics=("arbitrary",),
                                             vmem_limit_bytes=_Plan.vmem_limit),
        name="adaln",
    )(cvec, w_mod, b_mod)


def _kv_from_z(z_ckv, z_rope, kvg_ref, wukv_ref, cos_k, sin_k, k_ref, v_ref, rows):
    ckv = _rms(z_ckv) * kvg_ref[...]
    kv = _dot(ckv.astype(BF16), wukv_ref[...])
    k_rope = _rope(z_rope, cos_k, sin_k)
    for h in range(N_HEADS):
        sl = slice(h * HEAD_SLOT, (h + 1) * HEAD_SLOT)
        k_ref[0, rows, sl] = (kv[:, sl] + k_rope).astype(BF16)
    v_ref[0, rows, :] = kv[:, N_HEADS * HEAD_SLOT:].astype(BF16)


def _proj_sub_tile(j, n_sub, xp_ref, x_ref, xn_ref, mod_ref, wmla_ref, wconv_ref, qg_ref, wuq_ref, kvg_ref,
                   wukv_ref, cw_ref, cq_ref, sq_ref, ck_ref, sk_ref, q_ref, k_ref, v_ref, y_ref, u_scr):
    i = pl.program_id(1)
    n = pl.num_programs(1)
    sub = x_ref.shape[1] // n_sub
    cw = y_ref.shape[2]
    rows = slice(j * sub, (j + 1) * sub)
    main = slice(HALO, HALO + sub)
    c0 = Q_LORA_RANK
    c1 = c0 + KV_LORA_RANK
    c2 = c1 + LANES
    shift, scale = mod_ref[0, 0:1, :], mod_ref[0, 1:2, :]
    before = xp_ref[0] if j == 0 else x_ref[0, j * sub - HALO:j * sub, :]
    after = xn_ref[0] if j == n_sub - 1 else x_ref[0, (j + 1) * sub:(j + 1) * sub + HALO, :]
    xe = jnp.concatenate([before, x_ref[0, rows, :], after], axis=0)
    hmod = (_rms(xe) * (1.0 + scale) + shift).astype(BF16)
    z_mla = _dot(hmod, wmla_ref[...])[main]
    yield
    z_conv = _dot(hmod, wconv_ref[:, cw:])
    yield
    cq = _rms(z_mla[:, :c0]) * qg_ref[...]
    qf = _dot(cq.astype(BF16), wuq_ref[...])

    u = z_conv[:, :cw] * z_conv[:, cw:]
    row = lax.broadcasted_iota(jnp.int32, (sub + 2 * HALO, 1), 0)
    if j == 0:
        u = jnp.where((row >= HALO) | (i > 0), u, 0.0)
    if j == n_sub - 1:
        u = jnp.where((row < HALO + sub) | (i < n - 1), u, 0.0)
    u_scr[j] = u
    w = cw_ref[...]
    y = (w[0:1] * u_scr[j, HALO - 1:HALO - 1 + sub]
         + w[1:2] * u_scr[j, HALO:HALO + sub]
         + w[2:3] * u_scr[j, HALO + 1:HALO + 1 + sub])

    cos_q, sin_q = cq_ref[rows, :], sq_ref[rows, :]
    for h in range(N_HEADS):
        sl = slice(h * HEAD_SLOT, (h + 1) * HEAD_SLOT)
        q_ref[0, rows, sl] = _rope(qf[:, sl], cos_q, sin_q).astype(BF16)
    _kv_from_z(z_mla[:, c0:c1], z_mla[:, c1:c2], kvg_ref, wukv_ref, ck_ref[rows, :], sk_ref[rows, :],
               k_ref, v_ref, rows)
    yield
    gate_b = _dot(hmod, wconv_ref[:, :cw])[main]
    y_ref[0, rows, :] = (gate_b * y).astype(BF16)
    yield


def _proj_kernel(n_cast, *refs):
    n_in = 15
    ins, cast_in = refs[:n_in], refs[n_in:n_in + n_cast]
    outs = refs[n_in + n_cast:n_in + n_cast + 4]
    cast_out, u_scr = refs[n_in + n_cast + 4:n_in + 2 * n_cast + 4], refs[n_in + 2 * n_cast + 4]
    n_sub = _Plan.tok_sub_tiles
    tiles = [_proj_sub_tile(j, n_sub, *ins, *outs, u_scr) for j in range(n_sub)]
    for j in _Plan.tok_issue_order:
        next(tiles[j])
    for src, dst in zip(cast_in, cast_out):
        dst[...] = src[...].astype(BF16)


def _ctx_kv_kernel(x_ref, mod_ref, win_ref, kvg_ref, wukv_ref, ck_ref, sk_ref, k_ref, v_ref):
    shift, scale = mod_ref[0, 0:1, :], mod_ref[0, 1:2, :]
    hmod = (_rms(x_ref[0]) * (1.0 + scale) + shift).astype(BF16)
    z = _dot(hmod, win_ref[...])
    _kv_from_z(z[:, :KV_LORA_RANK], z[:, KV_LORA_RANK:], kvg_ref, wukv_ref, ck_ref[...], sk_ref[...],
               k_ref, v_ref, slice(None))


def _const_spec(shape):
    return pl.BlockSpec(shape, lambda *_: (0,) * len(shape))


def _cast_block_rows(rows, steps):
    per = -(-rows // steps)
    return -(-per // BF16_ROWS) * BF16_ROWS


def _project_tokens(x, mod, w_mla, w_conv, q_g, w_uq_p, kv_g, w_ukv_p, conv_w, tabs, weights_f32):
    b, s, d = x.shape
    tm = _Plan.tok_tile
    cw = conv_w.shape[1]
    nh = N_HEADS * HEAD_SLOT
    nv = N_HEADS * V_HEAD_DIM
    seq_tiles = s // tm
    steps = b * seq_tiles
    halo_blocks = tm // HALO
    last_halo = s // HALO - 1

    def cast_spec(w):
        rb = _cast_block_rows(w.shape[0], steps)
        assert w.shape[0] % rb == 0
        nblk = w.shape[0] // rb
        return pl.BlockSpec((rb, w.shape[1]),
                            lambda bi, i: (jnp.minimum((bi * seq_tiles + i) * nblk // steps, nblk - 1), 0))

    cast_specs = [cast_spec(w) for w in weights_f32]
    tab_spec = pl.BlockSpec((tm, LANES), lambda bi, i: (i, 0))
    tok = lambda w: pl.BlockSpec((1, tm, w), lambda bi, i: (bi, i, 0))
    return pl.pallas_call(
        functools.partial(_proj_kernel, len(weights_f32)),
        grid=(b, seq_tiles),
        in_specs=[
            pl.BlockSpec((1, HALO, d), lambda bi, i: (bi, jnp.maximum(i * halo_blocks - 1, 0), 0)),
            tok(d),
            pl.BlockSpec((1, HALO, d), lambda bi, i: (bi, jnp.minimum((i + 1) * halo_blocks, last_halo), 0)),
            pl.BlockSpec((1,) + mod.shape[1:], lambda bi, i: (bi, 0, 0)),
            _const_spec(w_mla.shape), _const_spec(w_conv.shape), _const_spec(q_g.shape), _const_spec(w_uq_p.shape),
            _const_spec(kv_g.shape), _const_spec(w_ukv_p.shape), _const_spec(conv_w.shape),
            tab_spec, tab_spec, tab_spec, tab_spec,
        ] + cast_specs,
        out_specs=[tok(nh), tok(nh), tok(nv), tok(cw)] + cast_specs,
        out_shape=[jax.ShapeDtypeStruct((b, s, nh), BF16), jax.ShapeDtypeStruct((b, s, nh), BF16),
                   jax.ShapeDtypeStruct((b, s, nv), BF16), jax.ShapeDtypeStruct((b, s, cw), BF16)]
                  + [jax.ShapeDtypeStruct(w.shape, BF16) for w in weights_f32],
        scratch_shapes=[pltpu.VMEM((_Plan.tok_sub_tiles, tm // _Plan.tok_sub_tiles + 2 * HALO, cw), F32)],
        compiler_params=pltpu.CompilerParams(dimension_semantics=("arbitrary", "arbitrary"),
                                             vmem_limit_bytes=_Plan.vmem_limit),
        name="token_proj",
    )(x, x, x, mod, w_mla, w_conv, q_g, w_uq_p, kv_g, w_ukv_p, conv_w, *tabs, *weights_f32)


def _project_ctx(ctx, mod, ctx_row, w_mla, kv_g, w_ukv_p, tabs):
    b, l, d = ctx.shape
    nh = N_HEADS * HEAD_SLOT
    nv = N_HEADS * V_HEAD_DIM
    kv_cols = 2 * LANES
    group = _Plan.ctx_batches_per_step
    assert b % group == 0
    rows = group * l
    tabs = [np.tile(t, (group, 1)) for t in tabs]
    tok = lambda w: pl.BlockSpec((1, rows, w), lambda bi: (bi, 0, 0))
    kc, vc = pl.pallas_call(
        _ctx_kv_kernel,
        grid=(b // group,),
        in_specs=[
            tok(d),
            pl.BlockSpec((1,) + mod.shape[1:], lambda bi: (ctx_row, 0, 0)),
            pl.BlockSpec((d, kv_cols), lambda bi: (0, Q_LORA_RANK // kv_cols)),
            _const_spec(kv_g.shape), _const_spec(w_ukv_p.shape),
            _const_spec(tabs[0].shape), _const_spec(tabs[1].shape),
        ],
        out_specs=[tok(nh), tok(nv)],
        out_shape=[jax.ShapeDtypeStruct((b // group, rows, nh), BF16),
                   jax.ShapeDtypeStruct((b // group, rows, nv), BF16)],
        compiler_params=pltpu.CompilerParams(dimension_semantics=("arbitrary",),
                                             vmem_limit_bytes=_Plan.vmem_limit),
        name="ctx_proj",
    )(ctx.reshape(b // group, rows, d), mod, w_mla, kv_g, w_ukv_p, *tabs)
    return kc.reshape(b, l, nh), vc.reshape(b, l, nv)


def _lane_tile_reduce(op, a):
    return functools.reduce(op, [a[:, t * LANES:(t + 1) * LANES] for t in range(a.shape[1] // LANES)])


def _key_chunks(main_ref, ctx_ref):
    s_len, tk = main_ref.shape[1], _Plan.kv_chunk
    return [(main_ref, c * tk, tk, c * tk) for c in range(s_len // tk)] + [(ctx_ref, 0, ctx_ref.shape[1], s_len)]


def _scores(q_ref, k_ref, kc_ref, head, s_scr):
    hs = slice(head * HEAD_SLOT, (head + 1) * HEAD_SLOT)
    q = q_ref[0, :, hs]
    m_run = None
    for ref, r0, rows, c0 in _key_chunks(k_ref, kc_ref):
        s = _dot_nt(q, ref[0, r0:r0 + rows, hs])
        s_scr[:, c0:c0 + rows] = s
        cm = _lane_tile_reduce(jnp.maximum, s)
        m_run = cm if m_run is None else jnp.maximum(m_run, cm)
    return m_run


def _weighted_values(s_scr, m_lanes, v_ref, vc_ref):
    m_rep = jnp.broadcast_to(jnp.max(m_lanes, axis=-1, keepdims=True), m_lanes.shape)
    acc = None
    for ref, r0, rows, c0 in _key_chunks(v_ref, vc_ref):
        tiles = [jnp.exp2(s_scr[:, c0 + t * LANES:c0 + (t + 1) * LANES] - m_rep) for t in range(rows // LANES)]
        rhs = jnp.concatenate([ref[0, r0:r0 + rows, :], jnp.ones((rows, LANES), BF16)], axis=1)
        pv = _dot(jnp.concatenate(tiles, axis=1).astype(BF16), rhs)
        acc = pv if acc is None else acc + pv
    return acc


def _attn_kernel(q_ref, k_ref, kc_ref, vp_ref, vcp_ref, vn_ref, vcn_ref, o_ref,
                 s0_scr, s1_scr, m1_scr, o0_scr):
    t = pl.program_id(0)

    @pl.when(t == 0)
    def _():
        s1_scr[...] = jnp.zeros_like(s1_scr)
        m1_scr[...] = jnp.zeros_like(m1_scr)
        o0_scr[...] = jnp.ones_like(o0_scr)

    m0 = _scores(q_ref, k_ref, kc_ref, 0, s0_scr)
    acc1 = _weighted_values(s1_scr, m1_scr[...], vp_ref, vcp_ref)
    acc0 = o0_scr[...]
    o0 = acc0[:, :LANES] / acc0[:, LANES:]
    o1 = acc1[:, :LANES] / acc1[:, LANES:]
    lane = lax.broadcasted_iota(jnp.int32, o0.shape, 1)
    o_ref[0] = jnp.where(lane < V_HEAD_DIM, o0, o1).astype(BF16)
    m1_scr[...] = _scores(q_ref, k_ref, kc_ref, 1, s1_scr)
    o0_scr[...] = _weighted_values(s0_scr, m0, vn_ref, vcn_ref)


def _attention(q, k, v, kc, vc):
    b, s, _ = q.shape
    l = kc.shape[1]
    tq = _Plan.q_tile
    groups = N_HEADS // HEADS_PER_STEP
    n_q = s // tq
    tiles = b * groups * n_q
    pair_w = HEADS_PER_STEP * HEAD_SLOT
    o_w = HEADS_PER_STEP * V_HEAD_DIM

    def decode(u):
        return u // (groups * n_q), (u // n_q) % groups, u % n_q

    def cur(t):
        return decode(jnp.minimum(t, tiles - 1))

    def prev(t):
        return decode(jnp.maximum(t - 1, 0))

    def keys_of(which, rows, width):
        return pl.BlockSpec((1, rows, width), lambda t: (which(t)[0], 0, which(t)[1]))

    def rows_of(which, width):
        return pl.BlockSpec((1, tq, width), lambda t: (which(t)[0], which(t)[2], which(t)[1]))

    return pl.pallas_call(
        _attn_kernel,
        grid=(tiles + 1,),
        in_specs=[rows_of(cur, pair_w), keys_of(cur, s, pair_w), keys_of(cur, l, pair_w),
                  keys_of(prev, s, o_w), keys_of(prev, l, o_w), keys_of(cur, s, o_w), keys_of(cur, l, o_w)],
        out_specs=rows_of(prev, o_w),
        out_shape=jax.ShapeDtypeStruct((b, s, N_HEADS * V_HEAD_DIM), BF16),
        scratch_shapes=[pltpu.VMEM((tq, s + l), F32), pltpu.VMEM((tq, s + l), F32),
                        pltpu.VMEM((tq, LANES), F32),
                        pltpu.VMEM((tq, 2 * LANES), F32)],
        compiler_params=pltpu.CompilerParams(dimension_semantics=("arbitrary",),
                                             vmem_limit_bytes=_Plan.vmem_limit),
        name="attention",
    )(q, k, kc, v, vc, v, vc)


def _mlp_kernel(x_ref, a_ref, y_ref, mod_ref, wo_ref, w1_ref, w2_ref, gf_ref, o_ref):
    gate1 = mod_ref[0, 2:3, :]
    shift2, scale2, gate2 = mod_ref[0, 3:4, :], mod_ref[0, 4:5, :], mod_ref[0, 5:6, :]
    sub = x_ref.shape[1] // _Plan.mlp_sub_tiles
    rows = [slice(r * sub, (r + 1) * sub) for r in range(_Plan.mlp_sub_tiles)]
    nv = a_ref.shape[2]
    mix = [_dot(a_ref[0, r, :], wo_ref[:nv, :]) + _dot(y_ref[0, r, :], wo_ref[nv:, :]) for r in rows]
    x1 = [x_ref[0, r, :] + gate1 * m for r, m in zip(rows, mix)]
    hmod = [(_rms(v) * (1.0 + scale2) + shift2).astype(BF16) for v in x1]
    fc = _Plan.ff_chunk
    acc = [None] * len(rows)
    for j in range(w1_ref.shape[1] // fc):
        u = [jnp.maximum(_dot(h, w1_ref[:, j * fc:(j + 1) * fc]), 0.0) for h in hmod]
        for r, v in enumerate(u):
            part = _dot((v * v).astype(BF16), w2_ref[j * fc:(j + 1) * fc, :])
            acc[r] = part if acc[r] is None else acc[r] + part
    for r, v, a in zip(rows, x1, acc):
        o_ref[0, r, :] = _rms(v + gate2 * a) * gf_ref[...]


def _out_proj_mlp(x, attn, conv, mod, wo, w1, w2, gf):
    b, s, d = x.shape
    tm = _Plan.mlp_tile
    tok = lambda w: pl.BlockSpec((1, tm, w), lambda bi, i: (bi, i, 0))
    return pl.pallas_call(
        _mlp_kernel,
        grid=(b, s // tm),
        in_specs=[tok(d), tok(attn.shape[2]), tok(conv.shape[2]),
                  pl.BlockSpec((1,) + mod.shape[1:], lambda bi, i: (bi, 0, 0)),
                  _const_spec(wo.shape), _const_spec(w1.shape), _const_spec(w2.shape), _const_spec(gf.shape)],
        out_specs=tok(d),
        out_shape=jax.ShapeDtypeStruct((b, s, d), F32),
        compiler_params=pltpu.CompilerParams(dimension_semantics=("arbitrary", "arbitrary"),
                                             vmem_limit_bytes=_Plan.vmem_limit),
        name="out_proj_mlp",
    )(x, attn, conv, mod, wo, w1, w2, gf)


def _rope_lanes(w):
    half = QK_ROPE_DIM // 4
    run = lambda k: w[..., k * half:(k + 1) * half]
    x1, x2 = [run(0), run(2)], [run(1), run(3)]
    return jnp.concatenate(x1 + x2 + x1 + x2, axis=-1)


def _rope_tables(rows, scale):
    half = QK_ROPE_DIM // 4
    f32 = np.float32
    pos = np.arange(rows * GRID_W)
    freqs = f32(ROPE_THETA) ** (-np.arange(0, 2 * half, 2, dtype=f32) / f32(2 * half))
    ang = np.concatenate([(pos // GRID_W).astype(f32)[:, None] * freqs,
                          (pos % GRID_W).astype(f32)[:, None] * freqs], axis=1)
    cos, sin = np.cos(ang), np.sin(ang)
    zeros = lambda w: np.zeros((pos.shape[0], w), f32)
    pad = LANES - QK_NOPE_DIM - 2 * ROPE_HALF
    cos_k = np.concatenate([zeros(QK_NOPE_DIM), cos, cos, zeros(pad)], axis=1)
    sin_k = np.concatenate([zeros(QK_NOPE_DIM), -sin, sin, zeros(pad)], axis=1)
    cos_q = np.concatenate([np.ones((pos.shape[0], QK_NOPE_DIM), f32), cos, cos, zeros(pad)], axis=1)
    return (cos_q * f32(scale)).astype(f32), (sin_k * f32(scale)).astype(f32), cos_k, sin_k


def kernel(x, c, ctx, c_ctx, w_mod, b_mod, w_in, q_norm_g, w_uq, kv_norm_g, w_ukv,
           conv_w, w_out, w_mlp1, w_mlp2, final_norm_g):
    b, s, d = x.shape
    l = ctx.shape[1]
    depth = w_mod.shape[0]
    assert depth == 1, "single-layer configuration"
    assert s % _Plan.tok_tile == 0 and s % _Plan.q_tile == 0 and s % _Plan.kv_chunk == 0
    assert s % GRID_W == 0 and l % LANES == 0 and N_HEADS % HEADS_PER_STEP == 0
    cw = conv_w.shape[2]
    assert w_in.shape[2] == MLA_IN + 3 * cw and d == N_HEADS * V_HEAD_DIM + cw

    ctx_row = b
    rows = -(-(b + 1) // SUBLANES_F32) * SUBLANES_F32
    cvec = jnp.concatenate([c, c_ctx[None, :], jnp.zeros((rows - b - 1, d), F32)], axis=0)
    mod = _adaln(cvec, w_mod[0], b_mod[0][None, :]).reshape(rows, 6, d)

    wi = w_in[0]
    n_lat = Q_LORA_RANK + KV_LORA_RANK
    w_mla = jnp.concatenate([wi[:, :n_lat], jnp.zeros((d, LANES - 2 * QK_ROPE_DIM), F32),
                             _rope_lanes(wi[:, n_lat:MLA_IN])], axis=1).astype(BF16)
    w_conv = wi[:, MLA_IN:].astype(BF16)
    wq = w_uq[0].reshape(Q_LORA_RANK, N_HEADS, QK_DIM)
    w_uq_p = jnp.concatenate([wq[:, :, :QK_NOPE_DIM], _rope_lanes(wq[:, :, QK_NOPE_DIM:])], axis=2)
    w_uq_p = w_uq_p.reshape(Q_LORA_RANK, N_HEADS * HEAD_SLOT).astype(BF16)
    wkv = w_ukv[0].reshape(KV_LORA_RANK, N_HEADS, QK_NOPE_DIM + V_HEAD_DIM)
    wk = jnp.concatenate([wkv[:, :, :QK_NOPE_DIM],
                          jnp.zeros((KV_LORA_RANK, N_HEADS, HEAD_SLOT - QK_NOPE_DIM), F32)], axis=2)
    w_ukv_p = jnp.concatenate([wk.reshape(KV_LORA_RANK, N_HEADS * HEAD_SLOT),
                               wkv[:, :, QK_NOPE_DIM:].reshape(KV_LORA_RANK, N_HEADS * V_HEAD_DIM)],
                              axis=1).astype(BF16)

    cos_q, sin_q, cos_k, sin_k = _rope_tables(s // GRID_W, ATTN_SCALE * LOG2E)
    lane = np.arange(LANES)
    ctx_cos = np.broadcast_to(((lane >= QK_NOPE_DIM) & (lane < QK_DIM)).astype(np.float32), (l, LANES))
    ctx_sin = np.zeros((l, LANES), np.float32)

    q, k, v, conv, w1, w2 = _project_tokens(x, mod, w_mla, w_conv, q_norm_g, w_uq_p, kv_norm_g, w_ukv_p,
                                            conv_w[0], (cos_q, sin_q, cos_k, sin_k), (w_mlp1[0], w_mlp2[0]))
    kc, vc = _project_ctx(ctx, mod, ctx_row, w_mla, kv_norm_g, w_ukv_p, (ctx_cos, ctx_sin))
    attn = _attention(q, k, v, kc, vc)

    return _out_proj_mlp(x, attn, conv, mod, w_out[0].astype(BF16), w1, w2, final_norm_g[None, :])
```

```python
import functools
import math

import jax
import jax.numpy as jnp
import numpy as np
from jax import lax
from jax.experimental import pallas as pl
from jax.experimental.pallas import tpu as pltpu

GRID_W = 64
N_HEADS = 8
QK_NOPE_DIM = 64
QK_ROPE_DIM = 32
V_HEAD_DIM = 64
Q_LORA_RANK = 256
KV_LORA_RANK = 128
CONV_K = 3
ROPE_THETA = 10000.0
EPS = 1e-6
QK_DIM = QK_NOPE_DIM + QK_ROPE_DIM
ROPE_HALF = QK_ROPE_DIM // 2
MLA_IN = Q_LORA_RANK + KV_LORA_RANK + QK_ROPE_DIM
ATTN_SCALE = 1.0 / math.sqrt(QK_DIM)
LOG2E = math.log2(math.e)

LANES = 128
SUBLANES_F32 = 8
BF16_ROWS = 16
VMEM_BYTES_V7X = 64 * 1024 * 1024

HEAD_SLOT = LANES
HEADS_PER_STEP = 2
HALO = SUBLANES_F32

F32 = jnp.float32
BF16 = jnp.bfloat16


class _Plan:
    tok_tile = 1024
    tok_sub_tiles = 2
    tok_issue_order = (0, 0, 1, 0, 1, 0, 1, 1)
    ctx_batches_per_step = 4
    mlp_tile = 1024
    mlp_sub_tiles = 4
    q_tile = 1024
    kv_chunk = 512
    ff_chunk = 1024
    mod_cols = 1536
    vmem_limit = 56 * 1024 * 1024
    assert vmem_limit < VMEM_BYTES_V7X


def _rms(x):
    return x * lax.rsqrt(jnp.mean(x * x, axis=-1, keepdims=True) + EPS)


def _dot(a, b):
    return jnp.dot(a, b, preferred_element_type=F32)


def _dot_nt(a, b):
    return lax.dot_general(a, b, (((1,), (1,)), ((), ())), preferred_element_type=F32)


def _rope(t, cos_tab, sin_tab):
    return t * cos_tab + pltpu.roll(t, LANES - ROPE_HALF, 1) * sin_tab


def _adaln_kernel(c_ref, w_ref, b_ref, o_ref):
    c = c_ref[...]
    a = (c / (1.0 + jnp.exp(-c))).astype(BF16)
    o_ref[...] = _dot(a, w_ref[...].astype(BF16)) + b_ref[...]


def _adaln(cvec, w_mod, b_mod):
    rows, d = cvec.shape
    n = w_mod.shape[1]
    tn = _Plan.mod_cols
    return pl.pallas_call(
        _adaln_kernel,
        grid=(n // tn,),
        in_specs=[pl.BlockSpec((rows, d), lambda j: (0, 0)),
                  pl.BlockSpec((d, tn), lambda j: (0, j)),
                  pl.BlockSpec((1, tn), lambda j: (0, j))],
        out_specs=pl.BlockSpec((rows, tn), lambda j: (0, j)),
        out_shape=jax.ShapeDtypeStruct((rows, n), F32),
        compiler_params=pltpu.CompilerParams(dimension_semantics=("arbitrary",),
                                             vmem_limit_bytes=_Plan.vmem_limit),
        name="adaln",
    )(cvec, w_mod, b_mod)


def _kv_from_z(z_ckv, z_rope, kvg_ref, wukv_ref, cos_k, sin_k, k_ref, v_ref, rows):
    ckv = _rms(z_ckv) * kvg_ref[...]
    kv = _dot(ckv.astype(BF16), wukv_ref[...])
    k_rope = _rope(z_rope, cos_k, sin_k)
    n_nope = N_HEADS * QK_NOPE_DIM
    low = lax.broadcasted_iota(jnp.int32, k_rope.shape, 1) < QK_NOPE_DIM
    for h in range(N_HEADS):
        pair = kv[:, (h // 2) * LANES:(h // 2 + 1) * LANES]
        slot = jnp.where(low, pair, k_rope) if h % 2 == 0 else jnp.where(low, k_rope, pair)
        k_ref[0, rows, h * HEAD_SLOT:(h + 1) * HEAD_SLOT] = slot.astype(BF16)
    v_ref[0, rows, :] = kv[:, n_nope:].astype(BF16)


def _proj_sub_tile(j, n_sub, xp_ref, x_ref, xn_ref, mod_ref, wmla_ref, wconv_ref, qg_ref, wuq_ref, kvg_ref,
                   wukv_ref, cw_ref, cq_ref, sq_ref, ck_ref, sk_ref, q_ref, k_ref, v_ref, y_ref, u_scr):
    i = pl.program_id(1)
    n = pl.num_programs(1)
    sub = x_ref.shape[1] // n_sub
    cw = y_ref.shape[2]
    rows = slice(j * sub, (j + 1) * sub)
    main = slice(HALO, HALO + sub)
    c0 = Q_LORA_RANK
    c1 = c0 + KV_LORA_RANK
    c2 = c1 + LANES
    shift, scale = mod_ref[0, 0:1, :], mod_ref[0, 1:2, :]
    before = xp_ref[0] if j == 0 else x_ref[0, j * sub - HALO:j * sub, :]
    after = xn_ref[0] if j == n_sub - 1 else x_ref[0, (j + 1) * sub:(j + 1) * sub + HALO, :]
    xe = jnp.concatenate([before, x_ref[0, rows, :], after], axis=0)
    hmod = (_rms(xe) * (1.0 + scale) + shift).astype(BF16)
    z_mla = _dot(hmod, wmla_ref[...])[main]
    yield
    z_conv = _dot(hmod, wconv_ref[:, cw:])
    yield
    cq = _rms(z_mla[:, :c0]) * qg_ref[...]
    qf = _dot(cq.astype(BF16), wuq_ref[...])

    u = z_conv[:, :cw] * z_conv[:, cw:]
    row = lax.broadcasted_iota(jnp.int32, (sub + 2 * HALO, 1), 0)
    if j == 0:
        u = jnp.where((row >= HALO) | (i > 0), u, 0.0)
    if j == n_sub - 1:
        u = jnp.where((row < HALO + sub) | (i < n - 1), u, 0.0)
    u_scr[j] = u
    w = cw_ref[...]
    y = (w[0:1] * u_scr[j, HALO - 1:HALO - 1 + sub]
         + w[1:2] * u_scr[j, HALO:HALO + sub]
         + w[2:3] * u_scr[j, HALO + 1:HALO + 1 + sub])

    for h in range(N_HEADS):
        sl = slice(h * HEAD_SLOT, (h + 1) * HEAD_SLOT)
        par = slice((h % 2) * LANES, (h % 2 + 1) * LANES)
        q_ref[0, rows, sl] = _rope(qf[:, sl], cq_ref[rows, par], sq_ref[rows, par]).astype(BF16)
    _kv_from_z(z_mla[:, c0:c1], z_mla[:, c1:c2], kvg_ref, wukv_ref, ck_ref[rows, :], sk_ref[rows, :],
               k_ref, v_ref, rows)
    yield
    gate_b = _dot(hmod, wconv_ref[:, :cw])[main]
    y_ref[0, rows, :] = (gate_b * y).astype(BF16)
    yield


def _proj_kernel(n_cast, *refs):
    n_in = 15
    ins, cast_in = refs[:n_in], refs[n_in:n_in + n_cast]
    outs = refs[n_in + n_cast:n_in + n_cast + 4]
    cast_out, u_scr = refs[n_in + n_cast + 4:n_in + 2 * n_cast + 4], refs[n_in + 2 * n_cast + 4]
    n_sub = _Plan.tok_sub_tiles
    tiles = [_proj_sub_tile(j, n_sub, *ins, *outs, u_scr) for j in range(n_sub)]
    for j in _Plan.tok_issue_order:
        next(tiles[j])
    for src, dst in zip(cast_in, cast_out):
        dst[...] = src[...].astype(BF16)


def _ctx_kv_kernel(x_ref, mod_ref, win_ref, kvg_ref, wukv_ref, ck_ref, sk_ref, k_ref, v_ref):
    shift, scale = mod_ref[0, 0:1, :], mod_ref[0, 1:2, :]
    hmod = (_rms(x_ref[0]) * (1.0 + scale) + shift).astype(BF16)
    z = _dot(hmod, win_ref[...])
    _kv_from_z(z[:, :KV_LORA_RANK], z[:, KV_LORA_RANK:], kvg_ref, wukv_ref, ck_ref[...], sk_ref[...],
               k_ref, v_ref, slice(None))


def _const_spec(shape):
    return pl.BlockSpec(shape, lambda *_: (0,) * len(shape))


def _cast_block_rows(rows, steps):
    per = -(-rows // steps)
    return -(-per // BF16_ROWS) * BF16_ROWS


def _project_tokens(x, mod, w_mla, w_conv, q_g, w_uq_p, kv_g, w_ukv_p, conv_w, tabs, weights_f32):
    b, s, d = x.shape
    tm = _Plan.tok_tile
    cw = conv_w.shape[1]
    nh = N_HEADS * HEAD_SLOT
    nv = N_HEADS * V_HEAD_DIM
    seq_tiles = s // tm
    steps = b * seq_tiles
    halo_blocks = tm // HALO
    last_halo = s // HALO - 1

    def cast_spec(w):
        rb = _cast_block_rows(w.shape[0], steps)
        assert w.shape[0] % rb == 0
        nblk = w.shape[0] // rb
        return pl.BlockSpec((rb, w.shape[1]),
                            lambda bi, i: (jnp.minimum((bi * seq_tiles + i) * nblk // steps, nblk - 1), 0))

    cast_specs = [cast_spec(w) for w in weights_f32]
    tab_spec = lambda w: pl.BlockSpec((tm, w), lambda bi, i: (i, 0))
    tok = lambda w: pl.BlockSpec((1, tm, w), lambda bi, i: (bi, i, 0))
    return pl.pallas_call(
        functools.partial(_proj_kernel, len(weights_f32)),
        grid=(b, seq_tiles),
        in_specs=[
            pl.BlockSpec((1, HALO, d), lambda bi, i: (bi, jnp.maximum(i * halo_blocks - 1, 0), 0)),
            tok(d),
            pl.BlockSpec((1, HALO, d), lambda bi, i: (bi, jnp.minimum((i + 1) * halo_blocks, last_halo), 0)),
            pl.BlockSpec((1,) + mod.shape[1:], lambda bi, i: (bi, 0, 0)),
            _const_spec(w_mla.shape), _const_spec(w_conv.shape), _const_spec(q_g.shape), _const_spec(w_uq_p.shape),
            _const_spec(kv_g.shape), _const_spec(w_ukv_p.shape), _const_spec(conv_w.shape),
            tab_spec(2 * LANES), tab_spec(2 * LANES), tab_spec(LANES), tab_spec(LANES),
        ] + cast_specs,
        out_specs=[tok(nh), tok(nh), tok(nv), tok(cw)] + cast_specs,
        out_shape=[jax.ShapeDtypeStruct((b, s, nh), BF16), jax.ShapeDtypeStruct((b, s, nh), BF16),
                   jax.ShapeDtypeStruct((b, s, nv), BF16), jax.ShapeDtypeStruct((b, s, cw), BF16)]
                  + [jax.ShapeDtypeStruct(w.shape, BF16) for w in weights_f32],
        scratch_shapes=[pltpu.VMEM((_Plan.tok_sub_tiles, tm // _Plan.tok_sub_tiles + 2 * HALO, cw), F32)],
        compiler_params=pltpu.CompilerParams(dimension_semantics=("arbitrary", "arbitrary"),
                                             vmem_limit_bytes=_Plan.vmem_limit),
        name="token_proj",
    )(x, x, x, mod, w_mla, w_conv, q_g, w_uq_p, kv_g, w_ukv_p, conv_w, *tabs, *weights_f32)


def _project_ctx(ctx, mod, ctx_row, w_mla, kv_g, w_ukv_p, tabs):
    b, l, d = ctx.shape
    nh = N_HEADS * HEAD_SLOT
    nv = N_HEADS * V_HEAD_DIM
    kv_cols = 2 * LANES
    group = _Plan.ctx_batches_per_step
    assert b % group == 0
    rows = group * l
    tabs = [np.tile(t, (group, 1)) for t in tabs]
    tok = lambda w: pl.BlockSpec((1, rows, w), lambda bi: (bi, 0, 0))
    kc, vc = pl.pallas_call(
        _ctx_kv_kernel,
        grid=(b // group,),
        in_specs=[
            tok(d),
            pl.BlockSpec((1,) + mod.shape[1:], lambda bi: (ctx_row, 0, 0)),
            pl.BlockSpec((d, kv_cols), lambda bi: (0, Q_LORA_RANK // kv_cols)),
            _const_spec(kv_g.shape), _const_spec(w_ukv_p.shape),
            _const_spec(tabs[0].shape), _const_spec(tabs[1].shape),
        ],
        out_specs=[tok(nh), tok(nv)],
        out_shape=[jax.ShapeDtypeStruct((b // group, rows, nh), BF16),
                   jax.ShapeDtypeStruct((b // group, rows, nv), BF16)],
        compiler_params=pltpu.CompilerParams(dimension_semantics=("arbitrary",),
                                             vmem_limit_bytes=_Plan.vmem_limit),
        name="ctx_proj",
    )(ctx.reshape(b // group, rows, d), mod, w_mla, kv_g, w_ukv_p, *tabs)
    return kc.reshape(b, l, nh), vc.reshape(b, l, nv)


def _lane_tile_reduce(op, a):
    return functools.reduce(op, [a[:, t * LANES:(t + 1) * LANES] for t in range(a.shape[1] // LANES)])


def _key_chunks(main_ref, ctx_ref):
    s_len, tk = main_ref.shape[1], _Plan.kv_chunk
    return [(main_ref, c * tk, tk, c * tk) for c in range(s_len // tk)] + [(ctx_ref, 0, ctx_ref.shape[1], s_len)]


def _scores(q_ref, k_ref, kc_ref, head, s_scr):
    hs = slice(head * HEAD_SLOT, (head + 1) * HEAD_SLOT)
    q = q_ref[0, :, hs]
    m_run = None
    for ref, r0, rows, c0 in _key_chunks(k_ref, kc_ref):
        s = _dot_nt(q, ref[0, r0:r0 + rows, hs])
        s_scr[:, c0:c0 + rows] = s
        cm = _lane_tile_reduce(jnp.maximum, s)
        m_run = cm if m_run is None else jnp.maximum(m_run, cm)
    return m_run


def _weighted_values(s_scr, m_lanes, v_ref, vc_ref):
    m_rep = jnp.broadcast_to(jnp.max(m_lanes, axis=-1, keepdims=True), m_lanes.shape)
    acc = None
    for ref, r0, rows, c0 in _key_chunks(v_ref, vc_ref):
        tiles = [jnp.exp2(s_scr[:, c0 + t * LANES:c0 + (t + 1) * LANES] - m_rep) for t in range(rows // LANES)]
        rhs = jnp.concatenate([ref[0, r0:r0 + rows, :], jnp.ones((rows, LANES), BF16)], axis=1)
        pv = _dot(jnp.concatenate(tiles, axis=1).astype(BF16), rhs)
        acc = pv if acc is None else acc + pv
    return acc


def _attn_kernel(q_ref, k_ref, kc_ref, vp_ref, vcp_ref, vn_ref, vcn_ref, o_ref,
                 s0_scr, s1_scr, m1_scr, o0_scr):
    t = pl.program_id(0)

    @pl.when(t == 0)
    def _():
        s1_scr[...] = jnp.zeros_like(s1_scr)
        m1_scr[...] = jnp.zeros_like(m1_scr)
        o0_scr[...] = jnp.ones_like(o0_scr)

    m0 = _scores(q_ref, k_ref, kc_ref, 0, s0_scr)
    acc1 = _weighted_values(s1_scr, m1_scr[...], vp_ref, vcp_ref)
    acc0 = o0_scr[...]
    o0 = acc0[:, :LANES] / acc0[:, LANES:]
    o1 = acc1[:, :LANES] / acc1[:, LANES:]
    lane = lax.broadcasted_iota(jnp.int32, o0.shape, 1)
    o_ref[0] = jnp.where(lane < V_HEAD_DIM, o0, o1).astype(BF16)
    m1_scr[...] = _scores(q_ref, k_ref, kc_ref, 1, s1_scr)
    o0_scr[...] = _weighted_values(s0_scr, m0, vn_ref, vcn_ref)


def _attention(q, k, v, kc, vc):
    b, s, _ = q.shape
    l = kc.shape[1]
    tq = _Plan.q_tile
    groups = N_HEADS // HEADS_PER_STEP
    n_q = s // tq
    tiles = b * groups * n_q
    pair_w = HEADS_PER_STEP * HEAD_SLOT
    o_w = HEADS_PER_STEP * V_HEAD_DIM

    def decode(u):
        return u // (groups * n_q), (u // n_q) % groups, u % n_q

    def cur(t):
        return decode(jnp.minimum(t, tiles - 1))

    def prev(t):
        return decode(jnp.maximum(t - 1, 0))

    def keys_of(which, rows, width):
        return pl.BlockSpec((1, rows, width), lambda t: (which(t)[0], 0, which(t)[1]))

    def rows_of(which, width):
        return pl.BlockSpec((1, tq, width), lambda t: (which(t)[0], which(t)[2], which(t)[1]))

    return pl.pallas_call(
        _attn_kernel,
        grid=(tiles + 1,),
        in_specs=[rows_of(cur, pair_w), keys_of(cur, s, pair_w), keys_of(cur, l, pair_w),
                  keys_of(prev, s, o_w), keys_of(prev, l, o_w), keys_of(cur, s, o_w), keys_of(cur, l, o_w)],
        out_specs=rows_of(prev, o_w),
        out_shape=jax.ShapeDtypeStruct((b, s, N_HEADS * V_HEAD_DIM), BF16),
        scratch_shapes=[pltpu.VMEM((tq, s + l), F32), pltpu.VMEM((tq, s + l), F32),
                        pltpu.VMEM((tq, LANES), F32),
                        pltpu.VMEM((tq, 2 * LANES), F32)],
        compiler_params=pltpu.CompilerParams(dimension_semantics=("arbitrary",),
                                             vmem_limit_bytes=_Plan.vmem_limit),
        name="attention",
    )(q, k, kc, v, vc, v, vc)


def _mlp_kernel(x_ref, a_ref, y_ref, mod_ref, wo_ref, w1_ref, w2_ref, gf_ref, o_ref):
    gate1 = mod_ref[0, 2:3, :]
    shift2, scale2, gate2 = mod_ref[0, 3:4, :], mod_ref[0, 4:5, :], mod_ref[0, 5:6, :]
    sub = x_ref.shape[1] // _Plan.mlp_sub_tiles
    rows = [slice(r * sub, (r + 1) * sub) for r in range(_Plan.mlp_sub_tiles)]
    nv = a_ref.shape[2]
    mix = [_dot(a_ref[0, r, :], wo_ref[:nv, :]) + _dot(y_ref[0, r, :], wo_ref[nv:, :]) for r in rows]
    x1 = [x_ref[0, r, :] + gate1 * m for r, m in zip(rows, mix)]
    hmod = [(_rms(v) * (1.0 + scale2) + shift2).astype(BF16) for v in x1]
    fc = _Plan.ff_chunk
    acc = [None] * len(rows)
    for j in range(w1_ref.shape[1] // fc):
        u = [jnp.maximum(_dot(h, w1_ref[:, j * fc:(j + 1) * fc]), 0.0) for h in hmod]
        for r, v in enumerate(u):
            part = _dot((v * v).astype(BF16), w2_ref[j * fc:(j + 1) * fc, :])
            acc[r] = part if acc[r] is None else acc[r] + part
    for r, v, a in zip(rows, x1, acc):
        o_ref[0, r, :] = _rms(v + gate2 * a) * gf_ref[...]


def _out_proj_mlp(x, attn, conv, mod, wo, w1, w2, gf):
    b, s, d = x.shape
    tm = _Plan.mlp_tile
    tok = lambda w: pl.BlockSpec((1, tm, w), lambda bi, i: (bi, i, 0))
    return pl.pallas_call(
        _mlp_kernel,
        grid=(b, s // tm),
        in_specs=[tok(d), tok(attn.shape[2]), tok(conv.shape[2]),
                  pl.BlockSpec((1,) + mod.shape[1:], lambda bi, i: (bi, 0, 0)),
                  _const_spec(wo.shape), _const_spec(w1.shape), _const_spec(w2.shape), _const_spec(gf.shape)],
        out_specs=tok(d),
        out_shape=jax.ShapeDtypeStruct((b, s, d), F32),
        compiler_params=pltpu.CompilerParams(dimension_semantics=("arbitrary", "arbitrary"),
                                             vmem_limit_bytes=_Plan.vmem_limit),
        name="out_proj_mlp",
    )(x, attn, conv, mod, wo, w1, w2, gf)


def _rope_lanes(w):
    half = QK_ROPE_DIM // 4
    run = lambda k: w[..., k * half:(k + 1) * half]
    x1, x2 = [run(0), run(2)], [run(1), run(3)]
    return jnp.concatenate(x1 + x2 + x1 + x2, axis=-1)


def _rope_tables(rows, scale):
    half = QK_ROPE_DIM // 4
    f32 = np.float32
    pos = np.arange(rows * GRID_W)
    freqs = f32(ROPE_THETA) ** (-np.arange(0, 2 * half, 2, dtype=f32) / f32(2 * half))
    ang = np.concatenate([(pos // GRID_W).astype(f32)[:, None] * freqs,
                          (pos % GRID_W).astype(f32)[:, None] * freqs], axis=1)
    cos, sin = np.cos(ang), np.sin(ang)
    zeros = lambda w: np.zeros((pos.shape[0], w), f32)
    pad = LANES - QK_NOPE_DIM - 2 * ROPE_HALF
    cos_r = np.concatenate([cos, cos, zeros(pad)], axis=1)
    sin_r = np.concatenate([-sin, sin, zeros(pad)], axis=1)
    ones = np.ones((pos.shape[0], QK_NOPE_DIM), f32)
    cos_q = np.concatenate([ones, cos_r, cos_r, ones], axis=1) * f32(scale)
    sin_q = np.concatenate([zeros(QK_NOPE_DIM), sin_r, sin_r, zeros(QK_NOPE_DIM)], axis=1) * f32(scale)
    cos_k = np.concatenate([cos_r, cos_r], axis=1)
    sin_k = np.concatenate([sin_r, sin_r], axis=1)
    return cos_q.astype(f32), sin_q.astype(f32), cos_k, sin_k


def kernel(x, c, ctx, c_ctx, w_mod, b_mod, w_in, q_norm_g, w_uq, kv_norm_g, w_ukv,
           conv_w, w_out, w_mlp1, w_mlp2, final_norm_g):
    b, s, d = x.shape
    l = ctx.shape[1]
    depth = w_mod.shape[0]
    assert depth == 1, "single-layer configuration"
    assert s % _Plan.tok_tile == 0 and s % _Plan.q_tile == 0 and s % _Plan.kv_chunk == 0
    assert s % GRID_W == 0 and l % LANES == 0 and N_HEADS % HEADS_PER_STEP == 0
    cw = conv_w.shape[2]
    assert w_in.shape[2] == MLA_IN + 3 * cw and d == N_HEADS * V_HEAD_DIM + cw

    ctx_row = b
    rows = -(-(b + 1) // SUBLANES_F32) * SUBLANES_F32
    cvec = jnp.concatenate([c, c_ctx[None, :], jnp.zeros((rows - b - 1, d), F32)], axis=0)
    mod = _adaln(cvec, w_mod[0], b_mod[0][None, :]).reshape(rows, 6, d)

    wi = w_in[0]
    n_lat = Q_LORA_RANK + KV_LORA_RANK
    w_rope = _rope_lanes(wi[:, n_lat:MLA_IN])
    w_mla = jnp.concatenate([wi[:, :n_lat], w_rope, w_rope], axis=1).astype(BF16)
    w_conv = wi[:, MLA_IN:].astype(BF16)
    wq = w_uq[0].reshape(Q_LORA_RANK, N_HEADS, QK_DIM)
    wq_nope, wq_rope = wq[:, :, :QK_NOPE_DIM], _rope_lanes(wq[:, :, QK_NOPE_DIM:])
    w_uq_p = jnp.stack([jnp.concatenate([wq_nope[:, 0::2], wq_rope[:, 0::2]], axis=2),
                        jnp.concatenate([wq_rope[:, 1::2], wq_nope[:, 1::2]], axis=2)], axis=2)
    w_uq_p = w_uq_p.reshape(Q_LORA_RANK, N_HEADS * HEAD_SLOT).astype(BF16)
    wkv = w_ukv[0].reshape(KV_LORA_RANK, N_HEADS, QK_NOPE_DIM + V_HEAD_DIM)
    w_ukv_p = jnp.concatenate([wkv[:, :, :QK_NOPE_DIM].reshape(KV_LORA_RANK, N_HEADS * QK_NOPE_DIM),
                               wkv[:, :, QK_NOPE_DIM:].reshape(KV_LORA_RANK, N_HEADS * V_HEAD_DIM)],
                              axis=1).astype(BF16)

    cos_q, sin_q, cos_k, sin_k = _rope_tables(s // GRID_W, ATTN_SCALE * LOG2E)
    lane = np.arange(LANES) % QK_NOPE_DIM
    ctx_cos = np.broadcast_to((lane < QK_ROPE_DIM).astype(np.float32), (l, LANES))
    ctx_sin = np.zeros((l, LANES), np.float32)

    q, k, v, conv, w1, w2 = _project_tokens(x, mod, w_mla, w_conv, q_norm_g, w_uq_p, kv_norm_g, w_ukv_p,
                                            conv_w[0], (cos_q, sin_q, cos_k, sin_k), (w_mlp1[0], w_mlp2[0]))
    kc, vc = _project_ctx(ctx, mod, ctx_row, w_mla, kv_norm_g, w_ukv_p, (ctx_cos, ctx_sin))
    attn = _attention(q, k, v, kc, vc)

    return _out_proj_mlp(x, attn, conv, mod, w_out[0].astype(BF16), w1, w2, final_norm_g[None, :])
```

```python
import functools
import math

import jax
import jax.numpy as jnp
import numpy as np
from jax import lax
from jax.experimental import pallas as pl
from jax.experimental.pallas import tpu as pltpu

GRID_W = 64
N_HEADS = 8
QK_NOPE_DIM = 64
QK_ROPE_DIM = 32
V_HEAD_DIM = 64
Q_LORA_RANK = 256
KV_LORA_RANK = 128
CONV_K = 3
ROPE_THETA = 10000.0
EPS = 1e-6
QK_DIM = QK_NOPE_DIM + QK_ROPE_DIM
ROPE_HALF = QK_ROPE_DIM // 2
MLA_IN = Q_LORA_RANK + KV_LORA_RANK + QK_ROPE_DIM
ATTN_SCALE = 1.0 / math.sqrt(QK_DIM)
LOG2E = math.log2(math.e)

LANES = 128
SUBLANES_F32 = 8
BF16_ROWS = 16
VMEM_BYTES_V7X = 64 * 1024 * 1024

HEAD_SLOT = LANES
HEADS_PER_STEP = 2
HALO = SUBLANES_F32

F32 = jnp.float32
BF16 = jnp.bfloat16


class _Plan:
    tok_tile = 1024
    tok_sub_tiles = 2
    tok_issue_order = (0, 0, 1, 0, 1, 0, 1, 1)
    ctx_batches_per_step = 4
    mlp_tile = 1024
    mlp_sub_tiles = 4
    q_tile = 1024
    kv_chunk = 512
    ff_chunk = 1024
    mod_cols = 3072
    vmem_limit = 56 * 1024 * 1024
    assert vmem_limit < VMEM_BYTES_V7X


def _rms(x):
    return x * lax.rsqrt(jnp.mean(x * x, axis=-1, keepdims=True) + EPS)


def _dot(a, b):
    return jnp.dot(a, b, preferred_element_type=F32)


def _dot_nt(a, b):
    return lax.dot_general(a, b, (((1,), (1,)), ((), ())), preferred_element_type=F32)


def _rope(t, cos_tab, sin_tab):
    return t * cos_tab + pltpu.roll(t, LANES - ROPE_HALF, 1) * sin_tab


def _adaln_kernel(c_ref, w_ref, b_ref, o_ref):
    c = c_ref[...]
    a = (c / (1.0 + jnp.exp(-c))).astype(BF16)
    o_ref[...] = _dot(a, w_ref[...].astype(BF16)) + b_ref[...]


def _adaln(cvec, w_mod, b_mod):
    rows, d = cvec.shape
    n = w_mod.shape[1]
    tn = _Plan.mod_cols
    return pl.pallas_call(
        _adaln_kernel,
        grid=(n // tn,),
        in_specs=[pl.BlockSpec((rows, d), lambda j: (0, 0)),
                  pl.BlockSpec((d, tn), lambda j: (0, j)),
                  pl.BlockSpec((1, tn), lambda j: (0, j))],
        out_specs=pl.BlockSpec((rows, tn), lambda j: (0, j)),
        out_shape=jax.ShapeDtypeStruct((rows, n), F32),
        compiler_params=pltpu.CompilerParams(dimension_semantics=("arbitrary",),
                                             vmem_limit_bytes=_Plan.vmem_limit),
        name="adaln",
    )(cvec, w_mod, b_mod)


def _kv_from_z(z_ckv, z_rope, kvg_ref, wukv_ref, cos_k, sin_k, k_ref, v_ref, rows):
    ckv = _rms(z_ckv) * kvg_ref[...]
    kv = _dot(ckv.astype(BF16), wukv_ref[...])
    k_rope = _rope(z_rope, cos_k, sin_k)
    n_nope = N_HEADS * QK_NOPE_DIM
    low = lax.broadcasted_iota(jnp.int32, k_rope.shape, 1) < QK_NOPE_DIM
    for h in range(N_HEADS):
        pair = kv[:, (h // 2) * LANES:(h // 2 + 1) * LANES]
        slot = jnp.where(low, pair, k_rope) if h % 2 == 0 else jnp.where(low, k_rope, pair)
        k_ref[0, rows, h * HEAD_SLOT:(h + 1) * HEAD_SLOT] = slot.astype(BF16)
    v_ref[0, rows, :] = kv[:, n_nope:].astype(BF16)


def _proj_sub_tile(j, n_sub, xp_ref, x_ref, xn_ref, mod_ref, wmla_ref, wconv_ref, qg_ref, wuq_ref, kvg_ref,
                   wukv_ref, cw_ref, cq_ref, sq_ref, ck_ref, sk_ref, q_ref, k_ref, v_ref, y_ref, u_scr):
    i = pl.program_id(1)
    n = pl.num_programs(1)
    sub = x_ref.shape[1] // n_sub
    cw = y_ref.shape[2]
    rows = slice(j * sub, (j + 1) * sub)
    main = slice(HALO, HALO + sub)
    c0 = Q_LORA_RANK
    c1 = c0 + KV_LORA_RANK
    c2 = c1 + LANES
    shift, scale = mod_ref[0, 0:1, :], mod_ref[0, 1:2, :]
    before = xp_ref[0] if j == 0 else x_ref[0, j * sub - HALO:j * sub, :]
    after = xn_ref[0] if j == n_sub - 1 else x_ref[0, (j + 1) * sub:(j + 1) * sub + HALO, :]
    xe = jnp.concatenate([before, x_ref[0, rows, :], after], axis=0)
    hmod = (_rms(xe) * (1.0 + scale) + shift).astype(BF16)
    z_mla = _dot(hmod, wmla_ref[...])[main]
    yield
    z_conv = _dot(hmod, wconv_ref[:, cw:])
    yield
    cq = _rms(z_mla[:, :c0]) * qg_ref[...]
    qf = _dot(cq.astype(BF16), wuq_ref[...])

    u = z_conv[:, :cw] * z_conv[:, cw:]
    row = lax.broadcasted_iota(jnp.int32, (sub + 2 * HALO, 1), 0)
    if j == 0:
        u = jnp.where((row >= HALO) | (i > 0), u, 0.0)
    if j == n_sub - 1:
        u = jnp.where((row < HALO + sub) | (i < n - 1), u, 0.0)
    u_scr[j] = u
    w = cw_ref[...]
    y = (w[0:1] * u_scr[j, HALO - 1:HALO - 1 + sub]
         + w[1:2] * u_scr[j, HALO:HALO + sub]
         + w[2:3] * u_scr[j, HALO + 1:HALO + 1 + sub])

    for h in range(N_HEADS):
        sl = slice(h * HEAD_SLOT, (h + 1) * HEAD_SLOT)
        par = slice((h % 2) * LANES, (h % 2 + 1) * LANES)
        q_ref[0, rows, sl] = _rope(qf[:, sl], cq_ref[rows, par], sq_ref[rows, par]).astype(BF16)
    _kv_from_z(z_mla[:, c0:c1], z_mla[:, c1:c2], kvg_ref, wukv_ref, ck_ref[rows, :], sk_ref[rows, :],
               k_ref, v_ref, rows)
    yield
    gate_b = _dot(hmod, wconv_ref[:, :cw])[main]
    y_ref[0, rows, :] = (gate_b * y).astype(BF16)
    yield


def _proj_kernel(n_cast, *refs):
    n_in = 15
    ins, cast_in = refs[:n_in], refs[n_in:n_in + n_cast]
    outs = refs[n_in + n_cast:n_in + n_cast + 4]
    cast_out, u_scr = refs[n_in + n_cast + 4:n_in + 2 * n_cast + 4], refs[n_in + 2 * n_cast + 4]
    n_sub = _Plan.tok_sub_tiles
    tiles = [_proj_sub_tile(j, n_sub, *ins, *outs, u_scr) for j in range(n_sub)]
    for j in _Plan.tok_issue_order:
        next(tiles[j])
    for src, dst in zip(cast_in, cast_out):
        dst[...] = src[...].astype(BF16)


def _ctx_kv_kernel(x_ref, mod_ref, win_ref, kvg_ref, wukv_ref, ck_ref, sk_ref, k_ref, v_ref):
    shift, scale = mod_ref[0, 0:1, :], mod_ref[0, 1:2, :]
    hmod = (_rms(x_ref[0]) * (1.0 + scale) + shift).astype(BF16)
    z = _dot(hmod, win_ref[...])
    _kv_from_z(z[:, :KV_LORA_RANK], z[:, KV_LORA_RANK:], kvg_ref, wukv_ref, ck_ref[...], sk_ref[...],
               k_ref, v_ref, slice(None))


def _const_spec(shape):
    return pl.BlockSpec(shape, lambda *_: (0,) * len(shape))


def _cast_block_rows(rows, steps):
    per = -(-rows // steps)
    return -(-per // BF16_ROWS) * BF16_ROWS


def _project_tokens(x, mod, w_mla, w_conv, q_g, w_uq_p, kv_g, w_ukv_p, conv_w, tabs, weights_f32):
    b, s, d = x.shape
    tm = _Plan.tok_tile
    cw = conv_w.shape[1]
    nh = N_HEADS * HEAD_SLOT
    nv = N_HEADS * V_HEAD_DIM
    seq_tiles = s // tm
    steps = b * seq_tiles
    halo_blocks = tm // HALO
    last_halo = s // HALO - 1

    def cast_spec(w):
        rb = _cast_block_rows(w.shape[0], steps)
        assert w.shape[0] % rb == 0
        nblk = w.shape[0] // rb
        return pl.BlockSpec((rb, w.shape[1]),
                            lambda bi, i: (jnp.minimum((bi * seq_tiles + i) * nblk // steps, nblk - 1), 0))

    cast_specs = [cast_spec(w) for w in weights_f32]
    tab_spec = lambda w: pl.BlockSpec((tm, w), lambda bi, i: (i, 0))
    tok = lambda w: pl.BlockSpec((1, tm, w), lambda bi, i: (bi, i, 0))
    return pl.pallas_call(
        functools.partial(_proj_kernel, len(weights_f32)),
        grid=(b, seq_tiles),
        in_specs=[
            pl.BlockSpec((1, HALO, d), lambda bi, i: (bi, jnp.maximum(i * halo_blocks - 1, 0), 0)),
            tok(d),
            pl.BlockSpec((1, HALO, d), lambda bi, i: (bi, jnp.minimum((i + 1) * halo_blocks, last_halo), 0)),
            pl.BlockSpec((1,) + mod.shape[1:], lambda bi, i: (bi, 0, 0)),
            _const_spec(w_mla.shape), _const_spec(w_conv.shape), _const_spec(q_g.shape), _const_spec(w_uq_p.shape),
            _const_spec(kv_g.shape), _const_spec(w_ukv_p.shape), _const_spec(conv_w.shape),
            tab_spec(2 * LANES), tab_spec(2 * LANES), tab_spec(LANES), tab_spec(LANES),
        ] + cast_specs,
        out_specs=[tok(nh), tok(nh), tok(nv), tok(cw)] + cast_specs,
        out_shape=[jax.ShapeDtypeStruct((b, s, nh), BF16), jax.ShapeDtypeStruct((b, s, nh), BF16),
                   jax.ShapeDtypeStruct((b, s, nv), BF16), jax.ShapeDtypeStruct((b, s, cw), BF16)]
                  + [jax.ShapeDtypeStruct(w.shape, BF16) for w in weights_f32],
        scratch_shapes=[pltpu.VMEM((_Plan.tok_sub_tiles, tm // _Plan.tok_sub_tiles + 2 * HALO, cw), F32)],
        compiler_params=pltpu.CompilerParams(dimension_semantics=("arbitrary", "arbitrary"),
                                             vmem_limit_bytes=_Plan.vmem_limit),
        name="token_proj",
    )(x, x, x, mod, w_mla, w_conv, q_g, w_uq_p, kv_g, w_ukv_p, conv_w, *tabs, *weights_f32)


def _project_ctx(ctx, mod, ctx_row, w_mla, kv_g, w_ukv_p, tabs):
    b, l, d = ctx.shape
    nh = N_HEADS * HEAD_SLOT
    nv = N_HEADS * V_HEAD_DIM
    kv_cols = 2 * LANES
    group = _Plan.ctx_batches_per_step
    assert b % group == 0
    rows = group * l
    tabs = [np.tile(t, (group, 1)) for t in tabs]
    tok = lambda w: pl.BlockSpec((1, rows, w), lambda bi: (bi, 0, 0))
    kc, vc = pl.pallas_call(
        _ctx_kv_kernel,
        grid=(b // group,),
        in_specs=[
            tok(d),
            pl.BlockSpec((1,) + mod.shape[1:], lambda bi: (ctx_row, 0, 0)),
            pl.BlockSpec((d, kv_cols), lambda bi: (0, Q_LORA_RANK // kv_cols)),
            _const_spec(kv_g.shape), _const_spec(w_ukv_p.shape),
            _const_spec(tabs[0].shape), _const_spec(tabs[1].shape),
        ],
        out_specs=[tok(nh), tok(nv)],
        out_shape=[jax.ShapeDtypeStruct((b // group, rows, nh), BF16),
                   jax.ShapeDtypeStruct((b // group, rows, nv), BF16)],
        compiler_params=pltpu.CompilerParams(dimension_semantics=("arbitrary",),
                                             vmem_limit_bytes=_Plan.vmem_limit),
        name="ctx_proj",
    )(ctx.reshape(b // group, rows, d), mod, w_mla, kv_g, w_ukv_p, *tabs)
    return kc.reshape(b, l, nh), vc.reshape(b, l, nv)


def _lane_tile_reduce(op, a):
    return functools.reduce(op, [a[:, t * LANES:(t + 1) * LANES] for t in range(a.shape[1] // LANES)])


def _key_chunks(main_ref, ctx_ref):
    s_len, tk = main_ref.shape[1], _Plan.kv_chunk
    return [(main_ref, c * tk, tk, c * tk) for c in range(s_len // tk)] + [(ctx_ref, 0, ctx_ref.shape[1], s_len)]


def _scores(q_ref, k_ref, kc_ref, head, s_scr):
    hs = slice(head * HEAD_SLOT, (head + 1) * HEAD_SLOT)
    q = q_ref[0, :, hs]
    m_run = None
    for ref, r0, rows, c0 in _key_chunks(k_ref, kc_ref):
        s = _dot_nt(q, ref[0, r0:r0 + rows, hs])
        s_scr[:, c0:c0 + rows] = s
        cm = _lane_tile_reduce(jnp.maximum, s)
        m_run = cm if m_run is None else jnp.maximum(m_run, cm)
    return m_run


def _weighted_values(s_scr, m_lanes, v_ref, vc_ref):
    m_rep = jnp.broadcast_to(jnp.max(m_lanes, axis=-1, keepdims=True), m_lanes.shape)
    acc = None
    for ref, r0, rows, c0 in _key_chunks(v_ref, vc_ref):
        tiles = [jnp.exp2(s_scr[:, c0 + t * LANES:c0 + (t + 1) * LANES] - m_rep) for t in range(rows // LANES)]
        rhs = jnp.concatenate([ref[0, r0:r0 + rows, :], jnp.ones((rows, LANES), BF16)], axis=1)
        pv = _dot(jnp.concatenate(tiles, axis=1).astype(BF16), rhs)
        acc = pv if acc is None else acc + pv
    return acc


def _attn_kernel(q_ref, k_ref, kc_ref, vp_ref, vcp_ref, vn_ref, vcn_ref, o_ref,
                 s0_scr, s1_scr, m1_scr, o0_scr):
    t = pl.program_id(0)

    @pl.when(t == 0)
    def _():
        s1_scr[...] = jnp.zeros_like(s1_scr)
        m1_scr[...] = jnp.zeros_like(m1_scr)
        o0_scr[...] = jnp.ones_like(o0_scr)

    m0 = _scores(q_ref, k_ref, kc_ref, 0, s0_scr)
    acc1 = _weighted_values(s1_scr, m1_scr[...], vp_ref, vcp_ref)
    acc0 = o0_scr[...]
    o0 = acc0[:, :LANES] / acc0[:, LANES:]
    o1 = acc1[:, :LANES] / acc1[:, LANES:]
    lane = lax.broadcasted_iota(jnp.int32, o0.shape, 1)
    o_ref[0] = jnp.where(lane < V_HEAD_DIM, o0, o1).astype(BF16)
    m1_scr[...] = _scores(q_ref, k_ref, kc_ref, 1, s1_scr)
    o0_scr[...] = _weighted_values(s0_scr, m0, vn_ref, vcn_ref)


def _attention(q, k, v, kc, vc):
    b, s, _ = q.shape
    l = kc.shape[1]
    tq = _Plan.q_tile
    groups = N_HEADS // HEADS_PER_STEP
    n_q = s // tq
    tiles = b * groups * n_q
    pair_w = HEADS_PER_STEP * HEAD_SLOT
    o_w = HEADS_PER_STEP * V_HEAD_DIM

    def decode(u):
        return u // (groups * n_q), (u // n_q) % groups, u % n_q

    def cur(t):
        return decode(jnp.minimum(t, tiles - 1))

    def prev(t):
        return decode(jnp.maximum(t - 1, 0))

    def keys_of(which, rows, width):
        return pl.BlockSpec((1, rows, width), lambda t: (which(t)[0], 0, which(t)[1]))

    def rows_of(which, width):
        return pl.BlockSpec((1, tq, width), lambda t: (which(t)[0], which(t)[2], which(t)[1]))

    return pl.pallas_call(
        _attn_kernel,
        grid=(tiles + 1,),
        in_specs=[rows_of(cur, pair_w), keys_of(cur, s, pair_w), keys_of(cur, l, pair_w),
                  keys_of(prev, s, o_w), keys_of(prev, l, o_w), keys_of(cur, s, o_w), keys_of(cur, l, o_w)],
        out_specs=rows_of(prev, o_w),
        out_shape=jax.ShapeDtypeStruct((b, s, N_HEADS * V_HEAD_DIM), BF16),
        scratch_shapes=[pltpu.VMEM((tq, s + l), F32), pltpu.VMEM((tq, s + l), F32),
                        pltpu.VMEM((tq, LANES), F32),
                        pltpu.VMEM((tq, 2 * LANES), F32)],
        compiler_params=pltpu.CompilerParams(dimension_semantics=("arbitrary",),
                                             vmem_limit_bytes=_Plan.vmem_limit),
        name="attention",
    )(q, k, kc, v, vc, v, vc)


def _mlp_kernel(x_ref, a_ref, y_ref, mod_ref, wo_ref, w1_ref, w2_ref, gf_ref, o_ref):
    gate1 = mod_ref[0, 2:3, :]
    shift2, scale2, gate2 = mod_ref[0, 3:4, :], mod_ref[0, 4:5, :], mod_ref[0, 5:6, :]
    sub = x_ref.shape[1] // _Plan.mlp_sub_tiles
    rows = [slice(r * sub, (r + 1) * sub) for r in range(_Plan.mlp_sub_tiles)]
    nv = a_ref.shape[2]
    mix = [_dot(a_ref[0, r, :], wo_ref[:nv, :]) + _dot(y_ref[0, r, :], wo_ref[nv:, :]) for r in rows]
    x1 = [x_ref[0, r, :] + gate1 * m for r, m in zip(rows, mix)]
    hmod = [(_rms(v) * (1.0 + scale2) + shift2).astype(BF16) for v in x1]
    fc = _Plan.ff_chunk
    acc = [None] * len(rows)
    for j in range(w1_ref.shape[1] // fc):
        u = [jnp.maximum(_dot(h, w1_ref[:, j * fc:(j + 1) * fc]), 0.0) for h in hmod]
        for r, v in enumerate(u):
            part = _dot((v * v).astype(BF16), w2_ref[j * fc:(j + 1) * fc, :])
            acc[r] = part if acc[r] is None else acc[r] + part
    for r, v, a in zip(rows, x1, acc):
        o_ref[0, r, :] = _rms(v + gate2 * a) * gf_ref[...]


def _out_proj_mlp(x, attn, conv, mod, wo, w1, w2, gf):
    b, s, d = x.shape
    tm = _Plan.mlp_tile
    tok = lambda w: pl.BlockSpec((1, tm, w), lambda bi, i: (bi, i, 0))
    return pl.pallas_call(
        _mlp_kernel,
        grid=(b, s // tm),
        in_specs=[tok(d), tok(attn.shape[2]), tok(conv.shape[2]),
                  pl.BlockSpec((1,) + mod.shape[1:], lambda bi, i: (bi, 0, 0)),
                  _const_spec(wo.shape), _const_spec(w1.shape), _const_spec(w2.shape), _const_spec(gf.shape)],
        out_specs=tok(d),
        out_shape=jax.ShapeDtypeStruct((b, s, d), F32),
        compiler_params=pltpu.CompilerParams(dimension_semantics=("arbitrary", "arbitrary"),
                                             vmem_limit_bytes=_Plan.vmem_limit),
        name="out_proj_mlp",
    )(x, attn, conv, mod, wo, w1, w2, gf)


def _rope_lanes(w):
    half = QK_ROPE_DIM // 4
    x1 = np.concatenate([np.arange(half), 2 * half + np.arange(half)])
    order = np.concatenate([x1, x1 + half, x1, x1 + half])
    select = np.zeros((QK_ROPE_DIM, order.size), np.float32)
    select[order, np.arange(order.size)] = 1.0
    return jnp.dot(w, select, precision=lax.Precision.HIGHEST)


def _rope_tables(rows, scale):
    half = QK_ROPE_DIM // 4
    f32 = np.float32
    pos = np.arange(rows * GRID_W)
    freqs = f32(ROPE_THETA) ** (-np.arange(0, 2 * half, 2, dtype=f32) / f32(2 * half))
    ang = np.concatenate([(pos // GRID_W).astype(f32)[:, None] * freqs,
                          (pos % GRID_W).astype(f32)[:, None] * freqs], axis=1)
    cos, sin = np.cos(ang), np.sin(ang)
    zeros = lambda w: np.zeros((pos.shape[0], w), f32)
    pad = LANES - QK_NOPE_DIM - 2 * ROPE_HALF
    cos_r = np.concatenate([cos, cos, zeros(pad)], axis=1)
    sin_r = np.concatenate([-sin, sin, zeros(pad)], axis=1)
    ones = np.ones((pos.shape[0], QK_NOPE_DIM), f32)
    cos_q = np.concatenate([ones, cos_r, cos_r, ones], axis=1) * f32(scale)
    sin_q = np.concatenate([zeros(QK_NOPE_DIM), sin_r, sin_r, zeros(QK_NOPE_DIM)], axis=1) * f32(scale)
    cos_k = np.concatenate([cos_r, cos_r], axis=1)
    sin_k = np.concatenate([sin_r, sin_r], axis=1)
    return cos_q.astype(f32), sin_q.astype(f32), cos_k, sin_k


def kernel(x, c, ctx, c_ctx, w_mod, b_mod, w_in, q_norm_g, w_uq, kv_norm_g, w_ukv,
           conv_w, w_out, w_mlp1, w_mlp2, final_norm_g):
    b, s, d = x.shape
    l = ctx.shape[1]
    depth = w_mod.shape[0]
    assert depth == 1, "single-layer configuration"
    assert s % _Plan.tok_tile == 0 and s % _Plan.q_tile == 0 and s % _Plan.kv_chunk == 0
    assert s % GRID_W == 0 and l % LANES == 0 and N_HEADS % HEADS_PER_STEP == 0
    cw = conv_w.shape[2]
    assert w_in.shape[2] == MLA_IN + 3 * cw and d == N_HEADS * V_HEAD_DIM + cw

    ctx_row = b
    rows = -(-(b + 1) // SUBLANES_F32) * SUBLANES_F32
    cvec = jnp.concatenate([c, c_ctx[None, :], jnp.zeros((rows - b - 1, d), F32)], axis=0)
    mod = _adaln(cvec, w_mod[0], b_mod[0][None, :]).reshape(rows, 6, d)

    wi = w_in[0]
    n_lat = Q_LORA_RANK + KV_LORA_RANK
    w_rope = _rope_lanes(wi[:, n_lat:MLA_IN])
    w_mla = jnp.concatenate([wi[:, :n_lat], w_rope, w_rope], axis=1).astype(BF16)
    w_conv = wi[:, MLA_IN:].astype(BF16)
    wq = w_uq[0].reshape(Q_LORA_RANK, N_HEADS, QK_DIM)
    wq_nope, wq_rope = wq[:, :, :QK_NOPE_DIM], _rope_lanes(wq[:, :, QK_NOPE_DIM:])
    w_uq_p = jnp.stack([jnp.concatenate([wq_nope[:, 0::2], wq_rope[:, 0::2]], axis=2),
                        jnp.concatenate([wq_rope[:, 1::2], wq_nope[:, 1::2]], axis=2)], axis=2)
    w_uq_p = w_uq_p.reshape(Q_LORA_RANK, N_HEADS * HEAD_SLOT).astype(BF16)
    wkv = w_ukv[0].reshape(KV_LORA_RANK, N_HEADS, QK_NOPE_DIM + V_HEAD_DIM)
    w_ukv_p = jnp.concatenate([wkv[:, :, :QK_NOPE_DIM].reshape(KV_LORA_RANK, N_HEADS * QK_NOPE_DIM),
                               wkv[:, :, QK_NOPE_DIM:].reshape(KV_LORA_RANK, N_HEADS * V_HEAD_DIM)],
                              axis=1).astype(BF16)

    cos_q, sin_q, cos_k, sin_k = _rope_tables(s // GRID_W, ATTN_SCALE * LOG2E)
    lane = np.arange(LANES) % QK_NOPE_DIM
    ctx_cos = np.broadcast_to((lane < QK_ROPE_DIM).astype(np.float32), (l, LANES))
    ctx_sin = np.zeros((l, LANES), np.float32)

    q, k, v, conv, w1, w2 = _project_tokens(x, mod, w_mla, w_conv, q_norm_g, w_uq_p, kv_norm_g, w_ukv_p,
                                            conv_w[0], (cos_q, sin_q, cos_k, sin_k), (w_mlp1[0], w_mlp2[0]))
    kc, vc = _project_ctx(ctx, mod, ctx_row, w_mla, kv_norm_g, w_ukv_p, (ctx_cos, ctx_sin))
    attn = _attention(q, k, v, kc, vc)

    return _out_proj_mlp(x, attn, conv, mod, w_out[0].astype(BF16), w1, w2, final_norm_g[None, :])
```

```python
import functools
import math

import jax
import jax.numpy as jnp
import numpy as np
from jax import lax
from jax.experimental import pallas as pl
from jax.experimental.pallas import tpu as pltpu

GRID_W = 64
N_HEADS = 8
QK_NOPE_DIM = 64
QK_ROPE_DIM = 32
V_HEAD_DIM = 64
Q_LORA_RANK = 256
KV_LORA_RANK = 128
CONV_K = 3
ROPE_THETA = 10000.0
EPS = 1e-6
QK_DIM = QK_NOPE_DIM + QK_ROPE_DIM
ROPE_HALF = QK_ROPE_DIM // 2
MLA_IN = Q_LORA_RANK + KV_LORA_RANK + QK_ROPE_DIM
ATTN_SCALE = 1.0 / math.sqrt(QK_DIM)
LOG2E = math.log2(math.e)

LANES = 128
SUBLANES_F32 = 8
BF16_ROWS = 16
VMEM_BYTES_V7X = 64 * 1024 * 1024

HEAD_SLOT = LANES
HEADS_PER_STEP = 2
HALO = SUBLANES_F32

F32 = jnp.float32
BF16 = jnp.bfloat16


class _Plan:
    tok_tile = 1024
    tok_sub_tiles = 2
    tok_issue_order = (0, 0, 1, 0, 1, 0, 1, 1)
    ctx_batches_per_step = 4
    mlp_tile = 1024
    mlp_sub_tiles = 4
    q_tile = 1024
    kv_chunk = 512
    ff_chunk = 1024
    mod_cols = 3072
    vmem_limit = 56 * 1024 * 1024
    assert vmem_limit < VMEM_BYTES_V7X


def _rms(x):
    return x * lax.rsqrt(jnp.mean(x * x, axis=-1, keepdims=True) + EPS)


def _dot(a, b):
    return jnp.dot(a, b, preferred_element_type=F32)


def _dot_nt(a, b):
    return lax.dot_general(a, b, (((1,), (1,)), ((), ())), preferred_element_type=F32)


def _rope(t, cos_tab, sin_tab):
    return t * cos_tab + pltpu.roll(t, LANES - ROPE_HALF, 1) * sin_tab


def _adaln_kernel(c_ref, w_ref, b_ref, o_ref):
    c = c_ref[...]
    a = (c / (1.0 + jnp.exp(-c))).astype(BF16)
    o_ref[...] = _dot(a, w_ref[...].astype(BF16)) + b_ref[...]


def _adaln(cvec, w_mod, b_mod):
    rows, d = cvec.shape
    n = w_mod.shape[1]
    tn = _Plan.mod_cols
    return pl.pallas_call(
        _adaln_kernel,
        grid=(n // tn,),
        in_specs=[pl.BlockSpec((rows, d), lambda j: (0, 0)),
                  pl.BlockSpec((d, tn), lambda j: (0, j)),
                  pl.BlockSpec((1, tn), lambda j: (0, j))],
        out_specs=pl.BlockSpec((rows, tn), lambda j: (0, j)),
        out_shape=jax.ShapeDtypeStruct((rows, n), F32),
        compiler_params=pltpu.CompilerParams(dimension_semantics=("arbitrary",),
                                             vmem_limit_bytes=_Plan.vmem_limit),
        name="adaln",
    )(cvec, w_mod, b_mod)


def _kv_from_z(z_ckv, z_rope, kvg_ref, wukv_ref, cos_k, sin_k, k_ref, v_ref, rows):
    ckv = _rms(z_ckv) * kvg_ref[...]
    kv = _dot(ckv.astype(BF16), wukv_ref[...])
    k_rope = _rope(z_rope, cos_k, sin_k)
    n_nope = N_HEADS * QK_NOPE_DIM
    low = lax.broadcasted_iota(jnp.int32, k_rope.shape, 1) < QK_NOPE_DIM
    for h in range(N_HEADS):
        pair = kv[:, (h // 2) * LANES:(h // 2 + 1) * LANES]
        slot = jnp.where(low, pair, k_rope) if h % 2 == 0 else jnp.where(low, k_rope, pair)
        k_ref[0, rows, h * HEAD_SLOT:(h + 1) * HEAD_SLOT] = slot.astype(BF16)
    v_ref[0, rows, :] = kv[:, n_nope:].astype(BF16)


def _proj_sub_tile(j, n_sub, xp_ref, x_ref, xn_ref, mod_ref, wmla_ref, wconv_ref, qg_ref, wuq_ref, kvg_ref,
                   wukv_ref, cw_ref, cq_ref, sq_ref, ck_ref, sk_ref, q_ref, k_ref, v_ref, y_ref, u_scr):
    i = pl.program_id(1)
    n = pl.num_programs(1)
    sub = x_ref.shape[1] // n_sub
    cw = y_ref.shape[2]
    rows = slice(j * sub, (j + 1) * sub)
    main = slice(HALO, HALO + sub)
    c0 = Q_LORA_RANK
    c1 = c0 + KV_LORA_RANK
    c2 = c1 + LANES
    shift, scale = mod_ref[0, 0:1, :], mod_ref[0, 1:2, :]
    before = xp_ref[0] if j == 0 else x_ref[0, j * sub - HALO:j * sub, :]
    after = xn_ref[0] if j == n_sub - 1 else x_ref[0, (j + 1) * sub:(j + 1) * sub + HALO, :]
    xe = jnp.concatenate([before, x_ref[0, rows, :], after], axis=0)
    hmod = (_rms(xe) * (1.0 + scale) + shift).astype(BF16)
    z_mla = _dot_nt(hmod, wmla_ref[...])[main]
    yield
    z_conv = _dot_nt(hmod, wconv_ref[cw:, :])
    yield
    cq = _rms(z_mla[:, :c0]) * qg_ref[...]
    qf = _dot(cq.astype(BF16), wuq_ref[...])

    u = z_conv[:, :cw] * z_conv[:, cw:]
    row = lax.broadcasted_iota(jnp.int32, (sub + 2 * HALO, 1), 0)
    if j == 0:
        u = jnp.where((row >= HALO) | (i > 0), u, 0.0)
    if j == n_sub - 1:
        u = jnp.where((row < HALO + sub) | (i < n - 1), u, 0.0)
    u_scr[j] = u
    w = cw_ref[...]
    y = (w[0:1] * u_scr[j, HALO - 1:HALO - 1 + sub]
         + w[1:2] * u_scr[j, HALO:HALO + sub]
         + w[2:3] * u_scr[j, HALO + 1:HALO + 1 + sub])

    for h in range(N_HEADS):
        sl = slice(h * HEAD_SLOT, (h + 1) * HEAD_SLOT)
        par = slice((h % 2) * LANES, (h % 2 + 1) * LANES)
        q_ref[0, rows, sl] = _rope(qf[:, sl], cq_ref[rows, par], sq_ref[rows, par]).astype(BF16)
    _kv_from_z(z_mla[:, c0:c1], z_mla[:, c1:c2], kvg_ref, wukv_ref, ck_ref[rows, :], sk_ref[rows, :],
               k_ref, v_ref, rows)
    yield
    gate_b = _dot_nt(hmod, wconv_ref[:cw, :])[main]
    y_ref[0, rows, :] = (gate_b * y).astype(BF16)
    yield


def _proj_kernel(n_cast, *refs):
    n_in = 15
    ins, cast_in = refs[:n_in], refs[n_in:n_in + n_cast]
    outs = refs[n_in + n_cast:n_in + n_cast + 4]
    cast_out, u_scr = refs[n_in + n_cast + 4:n_in + 2 * n_cast + 4], refs[n_in + 2 * n_cast + 4]
    n_sub = _Plan.tok_sub_tiles
    tiles = [_proj_sub_tile(j, n_sub, *ins, *outs, u_scr) for j in range(n_sub)]
    for j in _Plan.tok_issue_order:
        next(tiles[j])
    for src, dst in zip(cast_in, cast_out):
        dst[...] = src[...].astype(BF16)


def _ctx_kv_kernel(x_ref, mod_ref, win_ref, kvg_ref, wukv_ref, ck_ref, sk_ref, k_ref, v_ref):
    shift, scale = mod_ref[0, 0:1, :], mod_ref[0, 1:2, :]
    hmod = (_rms(x_ref[0]) * (1.0 + scale) + shift).astype(BF16)
    z = _dot_nt(hmod, win_ref[...])
    _kv_from_z(z[:, :KV_LORA_RANK], z[:, KV_LORA_RANK:], kvg_ref, wukv_ref, ck_ref[...], sk_ref[...],
               k_ref, v_ref, slice(None))


def _const_spec(shape):
    return pl.BlockSpec(shape, lambda *_: (0,) * len(shape))


def _cast_block_rows(rows, steps):
    per = -(-rows // steps)
    return -(-per // BF16_ROWS) * BF16_ROWS


def _project_tokens(x, mod, w_mla, w_conv, q_g, w_uq_p, kv_g, w_ukv_p, conv_w, tabs, weights_f32):
    b, s, d = x.shape
    tm = _Plan.tok_tile
    cw = conv_w.shape[1]
    nh = N_HEADS * HEAD_SLOT
    nv = N_HEADS * V_HEAD_DIM
    seq_tiles = s // tm
    steps = b * seq_tiles
    halo_blocks = tm // HALO
    last_halo = s // HALO - 1

    def cast_spec(w):
        rb = _cast_block_rows(w.shape[0], steps)
        assert w.shape[0] % rb == 0
        nblk = w.shape[0] // rb
        return pl.BlockSpec((rb, w.shape[1]),
                            lambda bi, i: (jnp.minimum((bi * seq_tiles + i) * nblk // steps, nblk - 1), 0))

    cast_specs = [cast_spec(w) for w in weights_f32]
    tab_spec = lambda w: pl.BlockSpec((tm, w), lambda bi, i: (i, 0))
    tok = lambda w: pl.BlockSpec((1, tm, w), lambda bi, i: (bi, i, 0))
    return pl.pallas_call(
        functools.partial(_proj_kernel, len(weights_f32)),
        grid=(b, seq_tiles),
        in_specs=[
            pl.BlockSpec((1, HALO, d), lambda bi, i: (bi, jnp.maximum(i * halo_blocks - 1, 0), 0)),
            tok(d),
            pl.BlockSpec((1, HALO, d), lambda bi, i: (bi, jnp.minimum((i + 1) * halo_blocks, last_halo), 0)),
            pl.BlockSpec((1,) + mod.shape[1:], lambda bi, i: (bi, 0, 0)),
            _const_spec(w_mla.shape), _const_spec(w_conv.shape), _const_spec(q_g.shape), _const_spec(w_uq_p.shape),
            _const_spec(kv_g.shape), _const_spec(w_ukv_p.shape), _const_spec(conv_w.shape),
            tab_spec(2 * LANES), tab_spec(2 * LANES), tab_spec(LANES), tab_spec(LANES),
        ] + cast_specs,
        out_specs=[tok(nh), tok(nh), tok(nv), tok(cw)] + cast_specs,
        out_shape=[jax.ShapeDtypeStruct((b, s, nh), BF16), jax.ShapeDtypeStruct((b, s, nh), BF16),
                   jax.ShapeDtypeStruct((b, s, nv), BF16), jax.ShapeDtypeStruct((b, s, cw), BF16)]
                  + [jax.ShapeDtypeStruct(w.shape, BF16) for w in weights_f32],
        scratch_shapes=[pltpu.VMEM((_Plan.tok_sub_tiles, tm // _Plan.tok_sub_tiles + 2 * HALO, cw), F32)],
        compiler_params=pltpu.CompilerParams(dimension_semantics=("arbitrary", "arbitrary"),
                                             vmem_limit_bytes=_Plan.vmem_limit),
        name="token_proj",
    )(x, x, x, mod, w_mla, w_conv, q_g, w_uq_p, kv_g, w_ukv_p, conv_w, *tabs, *weights_f32)


def _project_ctx(ctx, mod, ctx_row, w_mla, kv_g, w_ukv_p, tabs):
    b, l, d = ctx.shape
    nh = N_HEADS * HEAD_SLOT
    nv = N_HEADS * V_HEAD_DIM
    kv_cols = 2 * LANES
    group = _Plan.ctx_batches_per_step
    assert b % group == 0
    rows = group * l
    tabs = [np.tile(t, (group, 1)) for t in tabs]
    tok = lambda w: pl.BlockSpec((1, rows, w), lambda bi: (bi, 0, 0))
    kc, vc = pl.pallas_call(
        _ctx_kv_kernel,
        grid=(b // group,),
        in_specs=[
            tok(d),
            pl.BlockSpec((1,) + mod.shape[1:], lambda bi: (ctx_row, 0, 0)),
            pl.BlockSpec((kv_cols, d), lambda bi: (Q_LORA_RANK // kv_cols, 0)),
            _const_spec(kv_g.shape), _const_spec(w_ukv_p.shape),
            _const_spec(tabs[0].shape), _const_spec(tabs[1].shape),
        ],
        out_specs=[tok(nh), tok(nv)],
        out_shape=[jax.ShapeDtypeStruct((b // group, rows, nh), BF16),
                   jax.ShapeDtypeStruct((b // group, rows, nv), BF16)],
        compiler_params=pltpu.CompilerParams(dimension_semantics=("arbitrary",),
                                             vmem_limit_bytes=_Plan.vmem_limit),
        name="ctx_proj",
    )(ctx.reshape(b // group, rows, d), mod, w_mla, kv_g, w_ukv_p, *tabs)
    return kc.reshape(b, l, nh), vc.reshape(b, l, nv)


def _lane_tile_reduce(op, a):
    return functools.reduce(op, [a[:, t * LANES:(t + 1) * LANES] for t in range(a.shape[1] // LANES)])


def _key_chunks(main_ref, ctx_ref):
    s_len, tk = main_ref.shape[1], _Plan.kv_chunk
    return [(main_ref, c * tk, tk, c * tk) for c in range(s_len // tk)] + [(ctx_ref, 0, ctx_ref.shape[1], s_len)]


def _scores(q_ref, k_ref, kc_ref, head, s_scr):
    hs = slice(head * HEAD_SLOT, (head + 1) * HEAD_SLOT)
    q = q_ref[0, :, hs]
    m_run = None
    for ref, r0, rows, c0 in _key_chunks(k_ref, kc_ref):
        s = _dot_nt(q, ref[0, r0:r0 + rows, hs])
        s_scr[:, c0:c0 + rows] = s
        cm = _lane_tile_reduce(jnp.maximum, s)
        m_run = cm if m_run is None else jnp.maximum(m_run, cm)
    return m_run


def _weighted_values(s_scr, m_lanes, v_ref, vc_ref):
    m_rep = jnp.broadcast_to(jnp.max(m_lanes, axis=-1, keepdims=True), m_lanes.shape)
    acc = None
    for ref, r0, rows, c0 in _key_chunks(v_ref, vc_ref):
        tiles = [jnp.exp2(s_scr[:, c0 + t * LANES:c0 + (t + 1) * LANES] - m_rep) for t in range(rows // LANES)]
        rhs = jnp.concatenate([ref[0, r0:r0 + rows, :], jnp.ones((rows, LANES), BF16)], axis=1)
        pv = _dot(jnp.concatenate(tiles, axis=1).astype(BF16), rhs)
        acc = pv if acc is None else acc + pv
    return acc


def _attn_kernel(q_ref, k_ref, kc_ref, vp_ref, vcp_ref, vn_ref, vcn_ref, o_ref,
                 s0_scr, s1_scr, m1_scr, o0_scr):
    t = pl.program_id(0)

    @pl.when(t == 0)
    def _():
        s1_scr[...] = jnp.zeros_like(s1_scr)
        m1_scr[...] = jnp.zeros_like(m1_scr)
        o0_scr[...] = jnp.ones_like(o0_scr)

    m0 = _scores(q_ref, k_ref, kc_ref, 0, s0_scr)
    acc1 = _weighted_values(s1_scr, m1_scr[...], vp_ref, vcp_ref)
    acc0 = o0_scr[...]
    o0 = acc0[:, :LANES] / acc0[:, LANES:]
    o1 = acc1[:, :LANES] / acc1[:, LANES:]
    lane = lax.broadcasted_iota(jnp.int32, o0.shape, 1)
    o_ref[0] = jnp.where(lane < V_HEAD_DIM, o0, o1).astype(BF16)
    m1_scr[...] = _scores(q_ref, k_ref, kc_ref, 1, s1_scr)
    o0_scr[...] = _weighted_values(s0_scr, m0, vn_ref, vcn_ref)


def _attention(q, k, v, kc, vc):
    b, s, _ = q.shape
    l = kc.shape[1]
    tq = _Plan.q_tile
    groups = N_HEADS // HEADS_PER_STEP
    n_q = s // tq
    tiles = b * groups * n_q
    pair_w = HEADS_PER_STEP * HEAD_SLOT
    o_w = HEADS_PER_STEP * V_HEAD_DIM

    def decode(u):
        return u // (groups * n_q), (u // n_q) % groups, u % n_q

    def cur(t):
        return decode(jnp.minimum(t, tiles - 1))

    def prev(t):
        return decode(jnp.maximum(t - 1, 0))

    def keys_of(which, rows, width):
        return pl.BlockSpec((1, rows, width), lambda t: (which(t)[0], 0, which(t)[1]))

    def rows_of(which, width):
        return pl.BlockSpec((1, tq, width), lambda t: (which(t)[0], which(t)[2], which(t)[1]))

    return pl.pallas_call(
        _attn_kernel,
        grid=(tiles + 1,),
        in_specs=[rows_of(cur, pair_w), keys_of(cur, s, pair_w), keys_of(cur, l, pair_w),
                  keys_of(prev, s, o_w), keys_of(prev, l, o_w), keys_of(cur, s, o_w), keys_of(cur, l, o_w)],
        out_specs=rows_of(prev, o_w),
        out_shape=jax.ShapeDtypeStruct((b, s, N_HEADS * V_HEAD_DIM), BF16),
        scratch_shapes=[pltpu.VMEM((tq, s + l), F32), pltpu.VMEM((tq, s + l), F32),
                        pltpu.VMEM((tq, LANES), F32),
                        pltpu.VMEM((tq, 2 * LANES), F32)],
        compiler_params=pltpu.CompilerParams(dimension_semantics=("arbitrary",),
                                             vmem_limit_bytes=_Plan.vmem_limit),
        name="attention",
    )(q, k, kc, v, vc, v, vc)


def _mlp_kernel(x_ref, a_ref, y_ref, mod_ref, wo_ref, w1_ref, w2_ref, gf_ref, o_ref):
    gate1 = mod_ref[0, 2:3, :]
    shift2, scale2, gate2 = mod_ref[0, 3:4, :], mod_ref[0, 4:5, :], mod_ref[0, 5:6, :]
    sub = x_ref.shape[1] // _Plan.mlp_sub_tiles
    rows = [slice(r * sub, (r + 1) * sub) for r in range(_Plan.mlp_sub_tiles)]
    nv = a_ref.shape[2]
    mix = [_dot(a_ref[0, r, :], wo_ref[:nv, :]) + _dot(y_ref[0, r, :], wo_ref[nv:, :]) for r in rows]
    x1 = [x_ref[0, r, :] + gate1 * m for r, m in zip(rows, mix)]
    hmod = [(_rms(v) * (1.0 + scale2) + shift2).astype(BF16) for v in x1]
    fc = _Plan.ff_chunk
    acc = [None] * len(rows)
    for j in range(w1_ref.shape[1] // fc):
        u = [jnp.maximum(_dot(h, w1_ref[:, j * fc:(j + 1) * fc]), 0.0) for h in hmod]
        for r, v in enumerate(u):
            part = _dot((v * v).astype(BF16), w2_ref[j * fc:(j + 1) * fc, :])
            acc[r] = part if acc[r] is None else acc[r] + part
    for r, v, a in zip(rows, x1, acc):
        o_ref[0, r, :] = _rms(v + gate2 * a) * gf_ref[...]


def _out_proj_mlp(x, attn, conv, mod, wo, w1, w2, gf):
    b, s, d = x.shape
    tm = _Plan.mlp_tile
    tok = lambda w: pl.BlockSpec((1, tm, w), lambda bi, i: (bi, i, 0))
    return pl.pallas_call(
        _mlp_kernel,
        grid=(b, s // tm),
        in_specs=[tok(d), tok(attn.shape[2]), tok(conv.shape[2]),
                  pl.BlockSpec((1,) + mod.shape[1:], lambda bi, i: (bi, 0, 0)),
                  _const_spec(wo.shape), _const_spec(w1.shape), _const_spec(w2.shape), _const_spec(gf.shape)],
        out_specs=tok(d),
        out_shape=jax.ShapeDtypeStruct((b, s, d), F32),
        compiler_params=pltpu.CompilerParams(dimension_semantics=("arbitrary", "arbitrary"),
                                             vmem_limit_bytes=_Plan.vmem_limit),
        name="out_proj_mlp",
    )(x, attn, conv, mod, wo, w1, w2, gf)


def _rope_lanes(w, rows=False):
    half = QK_ROPE_DIM // 4
    x1 = np.concatenate([np.arange(half), 2 * half + np.arange(half)])
    order = np.concatenate([x1, x1 + half, x1, x1 + half])
    select = np.zeros((QK_ROPE_DIM, order.size), np.float32)
    select[order, np.arange(order.size)] = 1.0
    if rows:
        return jnp.dot(select.T, w, precision=lax.Precision.HIGHEST)
    return jnp.dot(w, select, precision=lax.Precision.HIGHEST)


def _rope_tables(rows, scale):
    half = QK_ROPE_DIM // 4
    f32 = np.float32
    pos = np.arange(rows * GRID_W)
    freqs = f32(ROPE_THETA) ** (-np.arange(0, 2 * half, 2, dtype=f32) / f32(2 * half))
    ang = np.concatenate([(pos // GRID_W).astype(f32)[:, None] * freqs,
                          (pos % GRID_W).astype(f32)[:, None] * freqs], axis=1)
    cos, sin = np.cos(ang), np.sin(ang)
    zeros = lambda w: np.zeros((pos.shape[0], w), f32)
    pad = LANES - QK_NOPE_DIM - 2 * ROPE_HALF
    cos_r = np.concatenate([cos, cos, zeros(pad)], axis=1)
    sin_r = np.concatenate([-sin, sin, zeros(pad)], axis=1)
    ones = np.ones((pos.shape[0], QK_NOPE_DIM), f32)
    cos_q = np.concatenate([ones, cos_r, cos_r, ones], axis=1) * f32(scale)
    sin_q = np.concatenate([zeros(QK_NOPE_DIM), sin_r, sin_r, zeros(QK_NOPE_DIM)], axis=1) * f32(scale)
    cos_k = np.concatenate([cos_r, cos_r], axis=1)
    sin_k = np.concatenate([sin_r, sin_r], axis=1)
    return cos_q.astype(f32), sin_q.astype(f32), cos_k, sin_k


def kernel(x, c, ctx, c_ctx, w_mod, b_mod, w_in, q_norm_g, w_uq, kv_norm_g, w_ukv,
           conv_w, w_out, w_mlp1, w_mlp2, final_norm_g):
    b, s, d = x.shape
    l = ctx.shape[1]
    depth = w_mod.shape[0]
    assert depth == 1, "single-layer configuration"
    assert s % _Plan.tok_tile == 0 and s % _Plan.q_tile == 0 and s % _Plan.kv_chunk == 0
    assert s % GRID_W == 0 and l % LANES == 0 and N_HEADS % HEADS_PER_STEP == 0
    cw = conv_w.shape[2]
    assert w_in.shape[2] == MLA_IN + 3 * cw and d == N_HEADS * V_HEAD_DIM + cw

    ctx_row = b
    rows = -(-(b + 1) // SUBLANES_F32) * SUBLANES_F32
    cvec = jnp.concatenate([c, c_ctx[None, :], jnp.zeros((rows - b - 1, d), F32)], axis=0)
    mod = _adaln(cvec, w_mod[0], b_mod[0][None, :]).reshape(rows, 6, d)

    wi = jnp.swapaxes(w_in[0], 0, 1)
    n_lat = Q_LORA_RANK + KV_LORA_RANK
    w_rope = _rope_lanes(wi[n_lat:MLA_IN], rows=True)
    w_mla = jnp.concatenate([wi[:n_lat], w_rope, w_rope], axis=0).astype(BF16)
    w_conv = wi[MLA_IN:].astype(BF16)
    wq = w_uq[0].reshape(Q_LORA_RANK, N_HEADS, QK_DIM)
    wq_nope, wq_rope = wq[:, :, :QK_NOPE_DIM], _rope_lanes(wq[:, :, QK_NOPE_DIM:])
    w_uq_p = jnp.stack([jnp.concatenate([wq_nope[:, 0::2], wq_rope[:, 0::2]], axis=2),
                        jnp.concatenate([wq_rope[:, 1::2], wq_nope[:, 1::2]], axis=2)], axis=2)
    w_uq_p = w_uq_p.reshape(Q_LORA_RANK, N_HEADS * HEAD_SLOT).astype(BF16)
    wkv = w_ukv[0].reshape(KV_LORA_RANK, N_HEADS, QK_NOPE_DIM + V_HEAD_DIM)
    w_ukv_p = jnp.concatenate([wkv[:, :, :QK_NOPE_DIM].reshape(KV_LORA_RANK, N_HEADS * QK_NOPE_DIM),
                               wkv[:, :, QK_NOPE_DIM:].reshape(KV_LORA_RANK, N_HEADS * V_HEAD_DIM)],
                              axis=1).astype(BF16)

    cos_q, sin_q, cos_k, sin_k = _rope_tables(s // GRID_W, ATTN_SCALE * LOG2E)
    lane = np.arange(LANES) % QK_NOPE_DIM
    ctx_cos = np.broadcast_to((lane < QK_ROPE_DIM).astype(np.float32), (l, LANES))
    ctx_sin = np.zeros((l, LANES), np.float32)

    q, k, v, conv, w1, w2 = _project_tokens(x, mod, w_mla, w_conv, q_norm_g, w_uq_p, kv_norm_g, w_ukv_p,
                                            conv_w[0], (cos_q, sin_q, cos_k, sin_k), (w_mlp1[0], w_mlp2[0]))
    kc, vc = _project_ctx(ctx, mod, ctx_row, w_mla, kv_norm_g, w_ukv_p, (ctx_cos, ctx_sin))
    attn = _attention(q, k, v, kc, vc)

    return _out_proj_mlp(x, attn, conv, mod, w_out[0].astype(BF16), w1, w2, final_norm_g[None, :])
```

```python
import functools
import math

import jax
import jax.numpy as jnp
import numpy as np
from jax import lax
from jax.experimental import pallas as pl
from jax.experimental.pallas import tpu as pltpu

GRID_W = 64
N_HEADS = 8
QK_NOPE_DIM = 64
QK_ROPE_DIM = 32
V_HEAD_DIM = 64
Q_LORA_RANK = 256
KV_LORA_RANK = 128
CONV_K = 3
ROPE_THETA = 10000.0
EPS = 1e-6
QK_DIM = QK_NOPE_DIM + QK_ROPE_DIM
ROPE_HALF = QK_ROPE_DIM // 2
MLA_IN = Q_LORA_RANK + KV_LORA_RANK + QK_ROPE_DIM
ATTN_SCALE = 1.0 / math.sqrt(QK_DIM)
LOG2E = math.log2(math.e)

LANES = 128
SUBLANES_F32 = 8
BF16_ROWS = 16
VMEM_BYTES_V7X = 64 * 1024 * 1024

HEAD_SLOT = LANES
HEADS_PER_STEP = 2
HALO = SUBLANES_F32

F32 = jnp.float32
BF16 = jnp.bfloat16


class _Plan:
    tok_tile = 1024
    tok_sub_tiles = 2
    tok_issue_order = (0, 0, 1, 0, 1, 0, 1, 1)
    ctx_batches_per_step = 4
    mlp_tile = 1024
    mlp_sub_tiles = 4
    q_tile = 1024
    kv_chunk = 512
    ff_chunk = 1024
    mod_cols = 3072
    vmem_limit = 56 * 1024 * 1024
    assert vmem_limit < VMEM_BYTES_V7X


def _rms(x):
    return x * lax.rsqrt(jnp.mean(x * x, axis=-1, keepdims=True) + EPS)


def _dot(a, b):
    return jnp.dot(a, b, preferred_element_type=F32)


def _dot_nt(a, b):
    return lax.dot_general(a, b, (((1,), (1,)), ((), ())), preferred_element_type=F32)


def _mod_vectors(mod_ref, row, first, count):
    d = mod_ref.shape[1] // 6
    vec = mod_ref[pl.ds(row, 1), :]
    return [vec[:, k * d:(k + 1) * d] for k in range(first, first + count)]


def _rope(t, cos_tab, sin_tab):
    return t * cos_tab + pltpu.roll(t, LANES - ROPE_HALF, 1) * sin_tab


def _adaln_kernel(c_ref, w_ref, b_ref, o_ref):
    c = c_ref[...]
    a = (c / (1.0 + jnp.exp(-c))).astype(BF16)
    o_ref[...] = _dot(a, w_ref[...].astype(BF16)) + b_ref[...]


def _adaln(cvec, w_mod, b_mod):
    rows, d = cvec.shape
    n = w_mod.shape[1]
    tn = _Plan.mod_cols
    return pl.pallas_call(
        _adaln_kernel,
        grid=(n // tn,),
        in_specs=[pl.BlockSpec((rows, d), lambda j: (0, 0)),
                  pl.BlockSpec((d, tn), lambda j: (0, j)),
                  pl.BlockSpec((1, tn), lambda j: (0, j))],
        out_specs=pl.BlockSpec((rows, tn), lambda j: (0, j)),
        out_shape=jax.ShapeDtypeStruct((rows, n), F32),
        compiler_params=pltpu.CompilerParams(dimension_semantics=("arbitrary",),
                                             vmem_limit_bytes=_Plan.vmem_limit),
        name="adaln",
    )(cvec, w_mod, b_mod)


def _kv_from_z(z_ckv, z_rope, kvg_ref, wukv_ref, cos_k, sin_k, k_ref, v_ref, rows):
    ckv = _rms(z_ckv) * kvg_ref[...]
    kv = _dot(ckv.astype(BF16), wukv_ref[...])
    k_rope = _rope(z_rope, cos_k, sin_k)
    n_nope = N_HEADS * QK_NOPE_DIM
    low = lax.broadcasted_iota(jnp.int32, k_rope.shape, 1) < QK_NOPE_DIM
    for h in range(N_HEADS):
        pair = kv[:, (h // 2) * LANES:(h // 2 + 1) * LANES]
        slot = jnp.where(low, pair, k_rope) if h % 2 == 0 else jnp.where(low, k_rope, pair)
        k_ref[0, rows, h * HEAD_SLOT:(h + 1) * HEAD_SLOT] = slot.astype(BF16)
    v_ref[0, rows, :] = kv[:, n_nope:].astype(BF16)


def _proj_sub_tile(j, n_sub, xp_ref, x_ref, xn_ref, mod_ref, wmla_ref, wconv_ref, qg_ref, wuq_ref, kvg_ref,
                   wukv_ref, cw_ref, cq_ref, sq_ref, ck_ref, sk_ref, q_ref, k_ref, v_ref, y_ref, u_scr):
    i = pl.program_id(1)
    n = pl.num_programs(1)
    sub = x_ref.shape[1] // n_sub
    cw = y_ref.shape[2]
    rows = slice(j * sub, (j + 1) * sub)
    main = slice(HALO, HALO + sub)
    c0 = Q_LORA_RANK
    c1 = c0 + KV_LORA_RANK
    c2 = c1 + LANES
    shift, scale = _mod_vectors(mod_ref, pl.program_id(0), 0, 2)
    before = xp_ref[0] if j == 0 else x_ref[0, j * sub - HALO:j * sub, :]
    after = xn_ref[0] if j == n_sub - 1 else x_ref[0, (j + 1) * sub:(j + 1) * sub + HALO, :]
    xe = jnp.concatenate([before, x_ref[0, rows, :], after], axis=0)
    hmod = (_rms(xe) * (1.0 + scale) + shift).astype(BF16)
    z_mla = _dot_nt(hmod, wmla_ref[...])[main]
    yield
    z_conv = _dot_nt(hmod, wconv_ref[cw:, :])
    yield
    cq = _rms(z_mla[:, :c0]) * qg_ref[...]
    qf = _dot(cq.astype(BF16), wuq_ref[...])

    u = z_conv[:, :cw] * z_conv[:, cw:]
    row = lax.broadcasted_iota(jnp.int32, (sub + 2 * HALO, 1), 0)
    if j == 0:
        u = jnp.where((row >= HALO) | (i > 0), u, 0.0)
    if j == n_sub - 1:
        u = jnp.where((row < HALO + sub) | (i < n - 1), u, 0.0)
    u_scr[j] = u
    w = cw_ref[...]
    y = (w[0:1] * u_scr[j, HALO - 1:HALO - 1 + sub]
         + w[1:2] * u_scr[j, HALO:HALO + sub]
         + w[2:3] * u_scr[j, HALO + 1:HALO + 1 + sub])

    for h in range(N_HEADS):
        sl = slice(h * HEAD_SLOT, (h + 1) * HEAD_SLOT)
        par = slice((h % 2) * LANES, (h % 2 + 1) * LANES)
        q_ref[0, rows, sl] = _rope(qf[:, sl], cq_ref[rows, par], sq_ref[rows, par]).astype(BF16)
    _kv_from_z(z_mla[:, c0:c1], z_mla[:, c1:c2], kvg_ref, wukv_ref, ck_ref[rows, :], sk_ref[rows, :],
               k_ref, v_ref, rows)
    yield
    gate_b = _dot_nt(hmod, wconv_ref[:cw, :])[main]
    y_ref[0, rows, :] = (gate_b * y).astype(BF16)
    yield


_PROJ_INPUT_REFS = 15


def _proj_kernel(n_cast, *refs):
    n_in = _PROJ_INPUT_REFS
    ins, cast_in = refs[:n_in], refs[n_in:n_in + n_cast]
    outs = refs[n_in + n_cast:n_in + n_cast + 4]
    cast_out, u_scr = refs[n_in + n_cast + 4:n_in + 2 * n_cast + 4], refs[n_in + 2 * n_cast + 4]
    n_sub = _Plan.tok_sub_tiles
    tiles = [_proj_sub_tile(j, n_sub, *ins, *outs, u_scr) for j in range(n_sub)]
    for j in _Plan.tok_issue_order:
        next(tiles[j])
    for src, dst in zip(cast_in, cast_out):
        dst[...] = src[...].astype(BF16)


def _ctx_kv_kernel(ctx_row, x_ref, mod_ref, win_ref, kvg_ref, wukv_ref, ck_ref, sk_ref, k_ref, v_ref):
    shift, scale = _mod_vectors(mod_ref, ctx_row, 0, 2)
    hmod = (_rms(x_ref[0]) * (1.0 + scale) + shift).astype(BF16)
    z = _dot_nt(hmod, win_ref[...])
    _kv_from_z(z[:, :KV_LORA_RANK], z[:, KV_LORA_RANK:], kvg_ref, wukv_ref, ck_ref[...], sk_ref[...],
               k_ref, v_ref, slice(None))


def _const_spec(shape):
    return pl.BlockSpec(shape, lambda *_: (0,) * len(shape))


def _cast_block_rows(rows, steps):
    per = -(-rows // steps)
    return -(-per // BF16_ROWS) * BF16_ROWS


def _project_tokens(x, mod, w_mla, w_conv, q_g, w_uq_p, kv_g, w_ukv_p, conv_w, tabs, weights_f32):
    b, s, d = x.shape
    tm = _Plan.tok_tile
    cw = conv_w.shape[1]
    nh = N_HEADS * HEAD_SLOT
    nv = N_HEADS * V_HEAD_DIM
    seq_tiles = s // tm
    steps = b * seq_tiles
    halo_blocks = tm // HALO
    last_halo = s // HALO - 1

    def cast_spec(w):
        rb = _cast_block_rows(w.shape[0], steps)
        assert w.shape[0] % rb == 0
        nblk = w.shape[0] // rb
        return pl.BlockSpec((rb, w.shape[1]),
                            lambda bi, i: (jnp.minimum((bi * seq_tiles + i) * nblk // steps, nblk - 1), 0))

    cast_specs = [cast_spec(w) for w in weights_f32]
    tab_spec = lambda w: pl.BlockSpec((tm, w), lambda bi, i: (i, 0))
    tok = lambda w: pl.BlockSpec((1, tm, w), lambda bi, i: (bi, i, 0))
    return pl.pallas_call(
        functools.partial(_proj_kernel, len(weights_f32)),
        grid=(b, seq_tiles),
        in_specs=[
            pl.BlockSpec((1, HALO, d), lambda bi, i: (bi, jnp.maximum(i * halo_blocks - 1, 0), 0)),
            tok(d),
            pl.BlockSpec((1, HALO, d), lambda bi, i: (bi, jnp.minimum((i + 1) * halo_blocks, last_halo), 0)),
            _const_spec(mod.shape),
            _const_spec(w_mla.shape), _const_spec(w_conv.shape), _const_spec(q_g.shape), _const_spec(w_uq_p.shape),
            _const_spec(kv_g.shape), _const_spec(w_ukv_p.shape), _const_spec(conv_w.shape),
            tab_spec(2 * LANES), tab_spec(2 * LANES), tab_spec(LANES), tab_spec(LANES),
        ] + cast_specs,
        out_specs=[tok(nh), tok(nh), tok(nv), tok(cw)] + cast_specs,
        out_shape=[jax.ShapeDtypeStruct((b, s, nh), BF16), jax.ShapeDtypeStruct((b, s, nh), BF16),
                   jax.ShapeDtypeStruct((b, s, nv), BF16), jax.ShapeDtypeStruct((b, s, cw), BF16)]
                  + [jax.ShapeDtypeStruct(w.shape, BF16) for w in weights_f32],
        scratch_shapes=[pltpu.VMEM((_Plan.tok_sub_tiles, tm // _Plan.tok_sub_tiles + 2 * HALO, cw), F32)],
        compiler_params=pltpu.CompilerParams(dimension_semantics=("arbitrary", "arbitrary"),
                                             vmem_limit_bytes=_Plan.vmem_limit),
        name="token_proj",
    )(x, x, x, mod, w_mla, w_conv, q_g, w_uq_p, kv_g, w_ukv_p, conv_w, *tabs, *weights_f32)


def _project_ctx(ctx, mod, ctx_row, w_mla, kv_g, w_ukv_p, tabs):
    b, l, d = ctx.shape
    nh = N_HEADS * HEAD_SLOT
    nv = N_HEADS * V_HEAD_DIM
    kv_cols = 2 * LANES
    group = _Plan.ctx_batches_per_step
    assert b % group == 0
    rows = group * l
    tabs = [np.tile(t, (group, 1)) for t in tabs]
    tok = lambda w: pl.BlockSpec((1, rows, w), lambda bi: (bi, 0, 0))
    kc, vc = pl.pallas_call(
        functools.partial(_ctx_kv_kernel, ctx_row),
        grid=(b // group,),
        in_specs=[
            tok(d),
            _const_spec(mod.shape),
            pl.BlockSpec((kv_cols, d), lambda bi: (Q_LORA_RANK // kv_cols, 0)),
            _const_spec(kv_g.shape), _const_spec(w_ukv_p.shape),
            _const_spec(tabs[0].shape), _const_spec(tabs[1].shape),
        ],
        out_specs=[tok(nh), tok(nv)],
        out_shape=[jax.ShapeDtypeStruct((b // group, rows, nh), BF16),
                   jax.ShapeDtypeStruct((b // group, rows, nv), BF16)],
        compiler_params=pltpu.CompilerParams(dimension_semantics=("arbitrary",),
                                             vmem_limit_bytes=_Plan.vmem_limit),
        name="ctx_proj",
    )(ctx.reshape(b // group, rows, d), mod, w_mla, kv_g, w_ukv_p, *tabs)
    return kc.reshape(b, l, nh), vc.reshape(b, l, nv)


def _lane_tile_reduce(op, a):
    return functools.reduce(op, [a[:, t * LANES:(t + 1) * LANES] for t in range(a.shape[1] // LANES)])


def _key_chunks(main_ref, ctx_ref):
    s_len, tk = main_ref.shape[1], _Plan.kv_chunk
    return [(main_ref, c * tk, tk, c * tk) for c in range(s_len // tk)] + [(ctx_ref, 0, ctx_ref.shape[1], s_len)]


def _scores(q_ref, k_ref, kc_ref, head, s_scr):
    hs = slice(head * HEAD_SLOT, (head + 1) * HEAD_SLOT)
    q = q_ref[0, :, hs]
    m_run = None
    for ref, r0, rows, c0 in _key_chunks(k_ref, kc_ref):
        s = _dot_nt(q, ref[0, r0:r0 + rows, hs])
        s_scr[:, c0:c0 + rows] = s
        cm = _lane_tile_reduce(jnp.maximum, s)
        m_run = cm if m_run is None else jnp.maximum(m_run, cm)
    return m_run


def _weighted_values(s_scr, m_lanes, v_ref, vc_ref):
    m_rep = jnp.broadcast_to(jnp.max(m_lanes, axis=-1, keepdims=True), m_lanes.shape)
    acc = None
    for ref, r0, rows, c0 in _key_chunks(v_ref, vc_ref):
        tiles = [jnp.exp2(s_scr[:, c0 + t * LANES:c0 + (t + 1) * LANES] - m_rep) for t in range(rows // LANES)]
        rhs = jnp.concatenate([ref[0, r0:r0 + rows, :], jnp.ones((rows, LANES), BF16)], axis=1)
        pv = _dot(jnp.concatenate(tiles, axis=1).astype(BF16), rhs)
        acc = pv if acc is None else acc + pv
    return acc


def _attn_kernel(q_ref, k_ref, kc_ref, vp_ref, vcp_ref, vn_ref, vcn_ref, o_ref,
                 s0_scr, s1_scr, m1_scr, o0_scr):
    t = pl.program_id(0)

    @pl.when(t == 0)
    def _():
        s1_scr[...] = jnp.zeros_like(s1_scr)
        m1_scr[...] = jnp.zeros_like(m1_scr)
        o0_scr[...] = jnp.ones_like(o0_scr)

    m0 = _scores(q_ref, k_ref, kc_ref, 0, s0_scr)
    acc1 = _weighted_values(s1_scr, m1_scr[...], vp_ref, vcp_ref)
    acc0 = o0_scr[...]
    o0 = acc0[:, :LANES] / acc0[:, LANES:]
    o1 = acc1[:, :LANES] / acc1[:, LANES:]
    lane = lax.broadcasted_iota(jnp.int32, o0.shape, 1)
    o_ref[0] = jnp.where(lane < V_HEAD_DIM, o0, o1).astype(BF16)
    m1_scr[...] = _scores(q_ref, k_ref, kc_ref, 1, s1_scr)
    o0_scr[...] = _weighted_values(s0_scr, m0, vn_ref, vcn_ref)


def _attention(q, k, v, kc, vc):
    b, s, _ = q.shape
    l = kc.shape[1]
    tq = _Plan.q_tile
    groups = N_HEADS // HEADS_PER_STEP
    n_q = s // tq
    tiles = b * groups * n_q
    pair_w = HEADS_PER_STEP * HEAD_SLOT
    o_w = HEADS_PER_STEP * V_HEAD_DIM

    def decode(u):
        return u // (groups * n_q), (u // n_q) % groups, u % n_q

    def cur(t):
        return decode(jnp.minimum(t, tiles - 1))

    def prev(t):
        return decode(jnp.maximum(t - 1, 0))

    def keys_of(which, rows, width):
        return pl.BlockSpec((1, rows, width), lambda t: (which(t)[0], 0, which(t)[1]))

    def rows_of(which, width):
        return pl.BlockSpec((1, tq, width), lambda t: (which(t)[0], which(t)[2], which(t)[1]))

    return pl.pallas_call(
        _attn_kernel,
        grid=(tiles + 1,),
        in_specs=[rows_of(cur, pair_w), keys_of(cur, s, pair_w), keys_of(cur, l, pair_w),
                  keys_of(prev, s, o_w), keys_of(prev, l, o_w), keys_of(cur, s, o_w), keys_of(cur, l, o_w)],
        out_specs=rows_of(prev, o_w),
        out_shape=jax.ShapeDtypeStruct((b, s, N_HEADS * V_HEAD_DIM), BF16),
        scratch_shapes=[pltpu.VMEM((tq, s + l), F32), pltpu.VMEM((tq, s + l), F32),
                        pltpu.VMEM((tq, LANES), F32),
                        pltpu.VMEM((tq, 2 * LANES), F32)],
        compiler_params=pltpu.CompilerParams(dimension_semantics=("arbitrary",),
                                             vmem_limit_bytes=_Plan.vmem_limit),
        name="attention",
    )(q, k, kc, v, vc, v, vc)


def _mlp_kernel(x_ref, a_ref, y_ref, mod_ref, wo_ref, w1_ref, w2_ref, gf_ref, o_ref):
    gate1, shift2, scale2, gate2 = _mod_vectors(mod_ref, pl.program_id(0), 2, 4)
    sub = x_ref.shape[1] // _Plan.mlp_sub_tiles
    rows = [slice(r * sub, (r + 1) * sub) for r in range(_Plan.mlp_sub_tiles)]
    nv = a_ref.shape[2]
    mix = [_dot(a_ref[0, r, :], wo_ref[:nv, :]) + _dot(y_ref[0, r, :], wo_ref[nv:, :]) for r in rows]
    x1 = [x_ref[0, r, :] + gate1 * m for r, m in zip(rows, mix)]
    hmod = [(_rms(v) * (1.0 + scale2) + shift2).astype(BF16) for v in x1]
    fc = _Plan.ff_chunk
    acc = [None] * len(rows)
    for j in range(w1_ref.shape[1] // fc):
        u = [jnp.maximum(_dot(h, w1_ref[:, j * fc:(j + 1) * fc]), 0.0) for h in hmod]
        for r, v in enumerate(u):
            part = _dot((v * v).astype(BF16), w2_ref[j * fc:(j + 1) * fc, :])
            acc[r] = part if acc[r] is None else acc[r] + part
    for r, v, a in zip(rows, x1, acc):
        o_ref[0, r, :] = _rms(v + gate2 * a) * gf_ref[...]


def _out_proj_mlp(x, attn, conv, mod, wo, w1, w2, gf):
    b, s, d = x.shape
    tm = _Plan.mlp_tile
    tok = lambda w: pl.BlockSpec((1, tm, w), lambda bi, i: (bi, i, 0))
    return pl.pallas_call(
        _mlp_kernel,
        grid=(b, s // tm),
        in_specs=[tok(d), tok(attn.shape[2]), tok(conv.shape[2]),
                  _const_spec(mod.shape),
                  _const_spec(wo.shape), _const_spec(w1.shape), _const_spec(w2.shape), _const_spec(gf.shape)],
        out_specs=tok(d),
        out_shape=jax.ShapeDtypeStruct((b, s, d), F32),
        compiler_params=pltpu.CompilerParams(dimension_semantics=("arbitrary", "arbitrary"),
                                             vmem_limit_bytes=_Plan.vmem_limit),
        name="out_proj_mlp",
    )(x, attn, conv, mod, wo, w1, w2, gf)


def _rope_lanes(w, rows=False):
    half = QK_ROPE_DIM // 4
    x1 = np.concatenate([np.arange(half), 2 * half + np.arange(half)])
    order = np.concatenate([x1, x1 + half, x1, x1 + half])
    select = np.zeros((QK_ROPE_DIM, order.size), np.float32)
    select[order, np.arange(order.size)] = 1.0
    if rows:
        return jnp.dot(select.T, w, precision=lax.Precision.HIGHEST)
    return jnp.dot(w, select, precision=lax.Precision.HIGHEST)


def _rope_tables(rows, scale):
    half = QK_ROPE_DIM // 4
    f32 = np.float32
    pos = np.arange(rows * GRID_W)
    freqs = f32(ROPE_THETA) ** (-np.arange(0, 2 * half, 2, dtype=f32) / f32(2 * half))
    ang = np.concatenate([(pos // GRID_W).astype(f32)[:, None] * freqs,
                          (pos % GRID_W).astype(f32)[:, None] * freqs], axis=1)
    cos, sin = np.cos(ang), np.sin(ang)
    zeros = lambda w: np.zeros((pos.shape[0], w), f32)
    pad = LANES - QK_NOPE_DIM - 2 * ROPE_HALF
    cos_r = np.concatenate([cos, cos, zeros(pad)], axis=1)
    sin_r = np.concatenate([-sin, sin, zeros(pad)], axis=1)
    ones = np.ones((pos.shape[0], QK_NOPE_DIM), f32)
    cos_q = np.concatenate([ones, cos_r, cos_r, ones], axis=1) * f32(scale)
    sin_q = np.concatenate([zeros(QK_NOPE_DIM), sin_r, sin_r, zeros(QK_NOPE_DIM)], axis=1) * f32(scale)
    cos_k = np.concatenate([cos_r, cos_r], axis=1)
    sin_k = np.concatenate([sin_r, sin_r], axis=1)
    return cos_q.astype(f32), sin_q.astype(f32), cos_k, sin_k


def kernel(x, c, ctx, c_ctx, w_mod, b_mod, w_in, q_norm_g, w_uq, kv_norm_g, w_ukv,
           conv_w, w_out, w_mlp1, w_mlp2, final_norm_g):
    b, s, d = x.shape
    l = ctx.shape[1]
    depth = w_mod.shape[0]
    assert depth == 1, "single-layer configuration"
    assert s % _Plan.tok_tile == 0 and s % _Plan.q_tile == 0 and s % _Plan.kv_chunk == 0
    assert s % _Plan.mlp_tile == 0 and s % GRID_W == 0 and l % LANES == 0 and N_HEADS % HEADS_PER_STEP == 0
    cw = conv_w.shape[2]
    assert conv_w.shape[1] == CONV_K and w_in.shape[2] == MLA_IN + 3 * cw and d == N_HEADS * V_HEAD_DIM + cw

    ctx_row = b
    rows = -(-(b + 1) // SUBLANES_F32) * SUBLANES_F32
    cvec = jnp.concatenate([c, c_ctx[None, :], jnp.zeros((rows - b - 1, d), F32)], axis=0)
    mod = _adaln(cvec, w_mod[0], b_mod[0][None, :])

    wi = jnp.swapaxes(w_in[0], 0, 1)
    n_lat = Q_LORA_RANK + KV_LORA_RANK
    w_rope = _rope_lanes(wi[n_lat:MLA_IN], rows=True)
    w_mla = jnp.concatenate([wi[:n_lat], w_rope, w_rope], axis=0).astype(BF16)
    w_conv = wi[MLA_IN:].astype(BF16)
    wq = w_uq[0].reshape(Q_LORA_RANK, N_HEADS, QK_DIM)
    wq_nope, wq_rope = wq[:, :, :QK_NOPE_DIM], _rope_lanes(wq[:, :, QK_NOPE_DIM:])
    w_uq_p = jnp.stack([jnp.concatenate([wq_nope[:, 0::2], wq_rope[:, 0::2]], axis=2),
                        jnp.concatenate([wq_rope[:, 1::2], wq_nope[:, 1::2]], axis=2)], axis=2)
    w_uq_p = w_uq_p.reshape(Q_LORA_RANK, N_HEADS * HEAD_SLOT).astype(BF16)
    wkv = w_ukv[0].reshape(KV_LORA_RANK, N_HEADS, QK_NOPE_DIM + V_HEAD_DIM)
    w_ukv_p = jnp.concatenate([wkv[:, :, :QK_NOPE_DIM].reshape(KV_LORA_RANK, N_HEADS * QK_NOPE_DIM),
                               wkv[:, :, QK_NOPE_DIM:].reshape(KV_LORA_RANK, N_HEADS * V_HEAD_DIM)],
                              axis=1).astype(BF16)

    cos_q, sin_q, cos_k, sin_k = _rope_tables(s // GRID_W, ATTN_SCALE * LOG2E)
    lane = np.arange(LANES) % QK_NOPE_DIM
    ctx_cos = np.broadcast_to((lane < QK_ROPE_DIM).astype(np.float32), (l, LANES))
    ctx_sin = np.zeros((l, LANES), np.float32)

    q, k, v, conv, w1, w2 = _project_tokens(x, mod, w_mla, w_conv, q_norm_g, w_uq_p, kv_norm_g, w_ukv_p,
                                            conv_w[0], (cos_q, sin_q, cos_k, sin_k), (w_mlp1[0], w_mlp2[0]))
    kc, vc = _project_ctx(ctx, mod, ctx_row, w_mla, kv_norm_g, w_ukv_p, (ctx_cos, ctx_sin))
    attn = _attention(q, k, v, kc, vc)

    return _out_proj_mlp(x, attn, conv, mod, w_out[0].astype(BF16), w1, w2, final_norm_g[None, :])
```

```python
import functools
import math

import jax
import jax.numpy as jnp
import numpy as np
from jax import lax
from jax.experimental import pallas as pl
from jax.experimental.pallas import tpu as pltpu

GRID_W = 64
N_HEADS = 8
QK_NOPE_DIM = 64
QK_ROPE_DIM = 32
V_HEAD_DIM = 64
Q_LORA_RANK = 256
KV_LORA_RANK = 128
CONV_K = 3
ROPE_THETA = 10000.0
EPS = 1e-6
QK_DIM = QK_NOPE_DIM + QK_ROPE_DIM
ROPE_HALF = QK_ROPE_DIM // 2
MLA_IN = Q_LORA_RANK + KV_LORA_RANK + QK_ROPE_DIM
ATTN_SCALE = 1.0 / math.sqrt(QK_DIM)
LOG2E = math.log2(math.e)

LANES = 128
SUBLANES_F32 = 8
BF16_ROWS = 16
VMEM_BYTES_V7X = 64 * 1024 * 1024

HEAD_SLOT = LANES
HEADS_PER_STEP = 2
HALO = SUBLANES_F32

F32 = jnp.float32
BF16 = jnp.bfloat16


class _Plan:
    tok_tile = 1024
    tok_sub_tiles = 2
    tok_issue_order = (0, 0, 1, 0, 1, 0, 1, 1)
    ctx_batches_per_step = 4
    mlp_tile = 1024
    mlp_sub_tiles = 4
    q_tile = 1024
    kv_chunk = 512
    ff_chunk = 1024
    mod_cols = 3072
    vmem_limit = 56 * 1024 * 1024
    assert vmem_limit < VMEM_BYTES_V7X


def _rms(x):
    return x * lax.rsqrt(jnp.mean(x * x, axis=-1, keepdims=True) + EPS)


def _dot(a, b):
    return jnp.dot(a, b, preferred_element_type=F32)


def _dot_nt(a, b):
    return lax.dot_general(a, b, (((1,), (1,)), ((), ())), preferred_element_type=F32)


def _mod_vectors(mod_ref, row, first, count):
    d = mod_ref.shape[1] // 6
    vec = mod_ref[pl.ds(row, 1), :]
    return [vec[:, k * d:(k + 1) * d] for k in range(first, first + count)]


def _rope(t, cos_tab, sin_tab):
    return t * cos_tab + pltpu.roll(t, LANES - ROPE_HALF, 1) * sin_tab


def _adaln_kernel(c_ref, w_ref, b_ref, o_ref):
    c = c_ref[...]
    a = (c / (1.0 + jnp.exp(-c))).astype(BF16)
    o_ref[...] = _dot(a, w_ref[...].astype(BF16)) + b_ref[...]


def _adaln(cvec, w_mod, b_mod):
    rows, d = cvec.shape
    n = w_mod.shape[1]
    tn = _Plan.mod_cols
    return pl.pallas_call(
        _adaln_kernel,
        grid=(n // tn,),
        in_specs=[pl.BlockSpec((rows, d), lambda j: (0, 0)),
                  pl.BlockSpec((d, tn), lambda j: (0, j)),
                  pl.BlockSpec((1, tn), lambda j: (0, j))],
        out_specs=pl.BlockSpec((rows, tn), lambda j: (0, j)),
        out_shape=jax.ShapeDtypeStruct((rows, n), F32),
        compiler_params=pltpu.CompilerParams(dimension_semantics=("arbitrary",),
                                             vmem_limit_bytes=_Plan.vmem_limit),
        name="adaln",
    )(cvec, w_mod, b_mod)


def _kv_from_z(z_ckv, z_rope, kvg_ref, wukv_ref, cos_k, sin_k, k_ref, v_ref, rows):
    ckv = _rms(z_ckv) * kvg_ref[...]
    kv = _dot(ckv.astype(BF16), wukv_ref[...])
    k_rope = _rope(z_rope, cos_k, sin_k)
    n_nope = N_HEADS * QK_NOPE_DIM
    low = lax.broadcasted_iota(jnp.int32, k_rope.shape, 1) < QK_NOPE_DIM
    for h in range(N_HEADS):
        pair = kv[:, (h // 2) * LANES:(h // 2 + 1) * LANES]
        slot = jnp.where(low, pair, k_rope) if h % 2 == 0 else jnp.where(low, k_rope, pair)
        k_ref[0, rows, h * HEAD_SLOT:(h + 1) * HEAD_SLOT] = slot.astype(BF16)
    v_ref[0, rows, :] = kv[:, n_nope:].astype(BF16)


def _proj_sub_tile(j, n_sub, xp_ref, x_ref, xn_ref, mod_ref, wmla_ref, wconv_ref, qg_ref, wuq_ref, kvg_ref,
                   wukv_ref, cw_ref, cq_ref, sq_ref, ck_ref, sk_ref, q_ref, k_ref, v_ref, y_ref, u_scr):
    i = pl.program_id(1)
    n = pl.num_programs(1)
    sub = x_ref.shape[1] // n_sub
    cw = y_ref.shape[2]
    rows = slice(j * sub, (j + 1) * sub)
    main = slice(HALO, HALO + sub)
    c0 = Q_LORA_RANK
    c1 = c0 + KV_LORA_RANK
    c2 = c1 + LANES
    shift, scale = _mod_vectors(mod_ref, pl.program_id(0), 0, 2)
    before = xp_ref[0] if j == 0 else x_ref[0, j * sub - HALO:j * sub, :]
    after = xn_ref[0] if j == n_sub - 1 else x_ref[0, (j + 1) * sub:(j + 1) * sub + HALO, :]
    xe = jnp.concatenate([before, x_ref[0, rows, :], after], axis=0)
    hmod = (_rms(xe) * (1.0 + scale) + shift).astype(BF16)
    z_mla = _dot_nt(hmod, wmla_ref[...])[main]
    yield
    z_conv = _dot_nt(hmod, wconv_ref[cw:, :])
    yield
    cq = _rms(z_mla[:, :c0]) * qg_ref[...]
    qf = _dot(cq.astype(BF16), wuq_ref[...])

    u = z_conv[:, :cw] * z_conv[:, cw:]
    row = lax.broadcasted_iota(jnp.int32, (sub + 2 * HALO, 1), 0)
    if j == 0:
        u = jnp.where((row >= HALO) | (i > 0), u, 0.0)
    if j == n_sub - 1:
        u = jnp.where((row < HALO + sub) | (i < n - 1), u, 0.0)
    u_scr[j] = u
    w = cw_ref[...]
    y = (w[0:1] * u_scr[j, HALO - 1:HALO - 1 + sub]
         + w[1:2] * u_scr[j, HALO:HALO + sub]
         + w[2:3] * u_scr[j, HALO + 1:HALO + 1 + sub])

    for h in range(N_HEADS):
        sl = slice(h * HEAD_SLOT, (h + 1) * HEAD_SLOT)
        par = slice((h % 2) * LANES, (h % 2 + 1) * LANES)
        q_ref[0, rows, sl] = _rope(qf[:, sl], cq_ref[rows, par], sq_ref[rows, par]).astype(BF16)
    _kv_from_z(z_mla[:, c0:c1], z_mla[:, c1:c2], kvg_ref, wukv_ref, ck_ref[rows, :], sk_ref[rows, :],
               k_ref, v_ref, rows)
    yield
    gate_b = _dot_nt(hmod, wconv_ref[:cw, :])[main]
    y_ref[0, rows, :] = (gate_b * y).astype(BF16)
    yield


_PROJ_INPUT_REFS = 15


def _proj_kernel(n_cast, *refs):
    n_in = _PROJ_INPUT_REFS
    ins, cast_in = refs[:n_in], refs[n_in:n_in + n_cast]
    outs = refs[n_in + n_cast:n_in + n_cast + 4]
    cast_out, u_scr = refs[n_in + n_cast + 4:n_in + 2 * n_cast + 4], refs[n_in + 2 * n_cast + 4]
    n_sub = _Plan.tok_sub_tiles
    tiles = [_proj_sub_tile(j, n_sub, *ins, *outs, u_scr) for j in range(n_sub)]
    for j in _Plan.tok_issue_order:
        next(tiles[j])
    for src, dst in zip(cast_in, cast_out):
        dst[...] = src[...].astype(BF16)


def _ctx_kv_kernel(ctx_row, x_ref, mod_ref, win_ref, kvg_ref, wukv_ref, ck_ref, sk_ref, k_all_ref, v_all_ref,
                   k_ref, v_ref):
    del k_all_ref, v_all_ref
    nb, l, d = x_ref.shape
    shift, scale = _mod_vectors(mod_ref, ctx_row, 0, 2)
    hmod = (_rms(x_ref[...].reshape(nb * l, d)) * (1.0 + scale) + shift).astype(BF16)
    z = _dot_nt(hmod, win_ref[...])
    for g in range(nb):
        zg = z[g * l:(g + 1) * l]
        _kv_from_z(zg[:, :KV_LORA_RANK], zg[:, KV_LORA_RANK:], kvg_ref, wukv_ref, ck_ref[...], sk_ref[...],
                   k_ref.at[pl.ds(g, 1)], v_ref.at[pl.ds(g, 1)], slice(None))


def _const_spec(shape):
    return pl.BlockSpec(shape, lambda *_: (0,) * len(shape))


def _cast_block_rows(rows, steps):
    per = -(-rows // steps)
    return -(-per // BF16_ROWS) * BF16_ROWS


def _project_tokens(x, mod, w_mla, w_conv, q_g, w_uq_p, kv_g, w_ukv_p, conv_w, tabs, weights_f32, ctx_len):
    b, s, d = x.shape
    tm = _Plan.tok_tile
    cw = conv_w.shape[1]
    nh = N_HEADS * HEAD_SLOT
    nv = N_HEADS * V_HEAD_DIM
    seq_tiles = s // tm
    steps = b * seq_tiles
    halo_blocks = tm // HALO
    last_halo = s // HALO - 1

    def cast_spec(w):
        rb = _cast_block_rows(w.shape[0], steps)
        assert w.shape[0] % rb == 0
        nblk = w.shape[0] // rb
        return pl.BlockSpec((rb, w.shape[1]),
                            lambda bi, i: (jnp.minimum((bi * seq_tiles + i) * nblk // steps, nblk - 1), 0))

    cast_specs = [cast_spec(w) for w in weights_f32]
    tab_spec = lambda w: pl.BlockSpec((tm, w), lambda bi, i: (i, 0))
    tok = lambda w: pl.BlockSpec((1, tm, w), lambda bi, i: (bi, i, 0))
    return pl.pallas_call(
        functools.partial(_proj_kernel, len(weights_f32)),
        grid=(b, seq_tiles),
        in_specs=[
            pl.BlockSpec((1, HALO, d), lambda bi, i: (bi, jnp.maximum(i * halo_blocks - 1, 0), 0)),
            tok(d),
            pl.BlockSpec((1, HALO, d), lambda bi, i: (bi, jnp.minimum((i + 1) * halo_blocks, last_halo), 0)),
            _const_spec(mod.shape),
            _const_spec(w_mla.shape), _const_spec(w_conv.shape), _const_spec(q_g.shape), _const_spec(w_uq_p.shape),
            _const_spec(kv_g.shape), _const_spec(w_ukv_p.shape), _const_spec(conv_w.shape),
            tab_spec(2 * LANES), tab_spec(2 * LANES), tab_spec(LANES), tab_spec(LANES),
        ] + cast_specs,
        out_specs=[tok(nh), tok(nh), tok(nv), tok(cw)] + cast_specs,
        out_shape=[jax.ShapeDtypeStruct((b, s, nh), BF16), jax.ShapeDtypeStruct((b, s + ctx_len, nh), BF16),
                   jax.ShapeDtypeStruct((b, s + ctx_len, nv), BF16), jax.ShapeDtypeStruct((b, s, cw), BF16)]
                  + [jax.ShapeDtypeStruct(w.shape, BF16) for w in weights_f32],
        scratch_shapes=[pltpu.VMEM((_Plan.tok_sub_tiles, tm // _Plan.tok_sub_tiles + 2 * HALO, cw), F32)],
        compiler_params=pltpu.CompilerParams(dimension_semantics=("arbitrary", "arbitrary"),
                                             vmem_limit_bytes=_Plan.vmem_limit),
        name="token_proj",
    )(x, x, x, mod, w_mla, w_conv, q_g, w_uq_p, kv_g, w_ukv_p, conv_w, *tabs, *weights_f32)


def _project_ctx(ctx, mod, ctx_row, w_mla, kv_g, w_ukv_p, tabs, k_all, v_all):
    b, l, d = ctx.shape
    s = k_all.shape[1] - l
    kv_cols = 2 * LANES
    group = _Plan.ctx_batches_per_step
    assert b % group == 0 and s % l == 0
    ctx_rows = lambda w: pl.BlockSpec((group, l, w), lambda bi: (bi, s // l, 0))
    in_place = pl.BlockSpec(memory_space=pl.ANY)
    return pl.pallas_call(
        functools.partial(_ctx_kv_kernel, ctx_row),
        grid=(b // group,),
        in_specs=[
            pl.BlockSpec((group, l, d), lambda bi: (bi, 0, 0)),
            _const_spec(mod.shape),
            pl.BlockSpec((kv_cols, d), lambda bi: (Q_LORA_RANK // kv_cols, 0)),
            _const_spec(kv_g.shape), _const_spec(w_ukv_p.shape),
            _const_spec(tabs[0].shape), _const_spec(tabs[1].shape),
            in_place, in_place,
        ],
        out_specs=[ctx_rows(k_all.shape[2]), ctx_rows(v_all.shape[2])],
        out_shape=[jax.ShapeDtypeStruct(k_all.shape, k_all.dtype), jax.ShapeDtypeStruct(v_all.shape, v_all.dtype)],
        input_output_aliases={7: 0, 8: 1},
        compiler_params=pltpu.CompilerParams(dimension_semantics=("arbitrary",),
                                             vmem_limit_bytes=_Plan.vmem_limit),
        name="ctx_proj",
    )(ctx, mod, w_mla, kv_g, w_ukv_p, *tabs, k_all, v_all)


def _lane_tile_reduce(op, a):
    return functools.reduce(op, [a[:, t * LANES:(t + 1) * LANES] for t in range(a.shape[1] // LANES)])


def _key_chunks(keys):
    tk = _Plan.kv_chunk
    chunks = [(c * tk, tk) for c in range(keys // tk)]
    return chunks + ([(keys // tk * tk, keys % tk)] if keys % tk else [])


def _scores(q_ref, k_ref, head, s_scr):
    hs = slice(head * HEAD_SLOT, (head + 1) * HEAD_SLOT)
    q = q_ref[0, :, hs]
    m_run = None
    for r0, rows in _key_chunks(k_ref.shape[1]):
        s = _dot_nt(q, k_ref[0, r0:r0 + rows, hs])
        s_scr[:, r0:r0 + rows] = s
        cm = _lane_tile_reduce(jnp.maximum, s)
        m_run = cm if m_run is None else jnp.maximum(m_run, cm)
    return m_run


def _weighted_values(s_scr, m_lanes, v_ref):
    m_rep = jnp.broadcast_to(jnp.max(m_lanes, axis=-1, keepdims=True), m_lanes.shape)
    acc = None
    for r0, rows in _key_chunks(v_ref.shape[1]):
        tiles = [jnp.exp2(s_scr[:, r0 + t * LANES:r0 + (t + 1) * LANES] - m_rep) for t in range(rows // LANES)]
        rhs = jnp.concatenate([v_ref[0, r0:r0 + rows, :], jnp.ones((rows, LANES), BF16)], axis=1)
        pv = _dot(jnp.concatenate(tiles, axis=1).astype(BF16), rhs)
        acc = pv if acc is None else acc + pv
    return acc


def _attn_kernel(q_ref, k_ref, vp_ref, vn_ref, o_ref, s0_scr, s1_scr, m1_scr, o0_scr):
    t = pl.program_id(0)

    @pl.when(t == 0)
    def _():
        s1_scr[...] = jnp.zeros_like(s1_scr)
        m1_scr[...] = jnp.zeros_like(m1_scr)
        o0_scr[...] = jnp.ones_like(o0_scr)

    m0 = _scores(q_ref, k_ref, 0, s0_scr)
    acc1 = _weighted_values(s1_scr, m1_scr[...], vp_ref)
    acc0 = o0_scr[...]
    o0 = acc0[:, :LANES] / acc0[:, LANES:]
    o1 = acc1[:, :LANES] / acc1[:, LANES:]
    lane = lax.broadcasted_iota(jnp.int32, o0.shape, 1)
    o_ref[0] = jnp.where(lane < V_HEAD_DIM, o0, o1).astype(BF16)
    m1_scr[...] = _scores(q_ref, k_ref, 1, s1_scr)
    o0_scr[...] = _weighted_values(s0_scr, m0, vn_ref)


def _attention(q, k, v):
    b, s, _ = q.shape
    keys = k.shape[1]
    tq = _Plan.q_tile
    groups = N_HEADS // HEADS_PER_STEP
    n_q = s // tq
    tiles = b * groups * n_q
    pair_w = HEADS_PER_STEP * HEAD_SLOT
    o_w = HEADS_PER_STEP * V_HEAD_DIM

    def decode(u):
        return u // (groups * n_q), (u // n_q) % groups, u % n_q

    def cur(t):
        return decode(jnp.minimum(t, tiles - 1))

    def prev(t):
        return decode(jnp.maximum(t - 1, 0))

    def keys_of(which, width):
        return pl.BlockSpec((1, keys, width), lambda t: (which(t)[0], 0, which(t)[1]))

    def rows_of(which, width):
        return pl.BlockSpec((1, tq, width), lambda t: (which(t)[0], which(t)[2], which(t)[1]))

    return pl.pallas_call(
        _attn_kernel,
        grid=(tiles + 1,),
        in_specs=[rows_of(cur, pair_w), keys_of(cur, pair_w), keys_of(prev, o_w), keys_of(cur, o_w)],
        out_specs=rows_of(prev, o_w),
        out_shape=jax.ShapeDtypeStruct((b, s, N_HEADS * V_HEAD_DIM), BF16),
        scratch_shapes=[pltpu.VMEM((tq, keys), F32), pltpu.VMEM((tq, keys), F32),
                        pltpu.VMEM((tq, LANES), F32),
                        pltpu.VMEM((tq, 2 * LANES), F32)],
        compiler_params=pltpu.CompilerParams(dimension_semantics=("arbitrary",),
                                             vmem_limit_bytes=_Plan.vmem_limit),
        name="attention",
    )(q, k, v, v)


def _mlp_kernel(x_ref, a_ref, y_ref, mod_ref, wo_ref, w1_ref, w2_ref, gf_ref, o_ref):
    gate1, shift2, scale2, gate2 = _mod_vectors(mod_ref, pl.program_id(0), 2, 4)
    sub = x_ref.shape[1] // _Plan.mlp_sub_tiles
    rows = [slice(r * sub, (r + 1) * sub) for r in range(_Plan.mlp_sub_tiles)]
    nv = a_ref.shape[2]
    mix = [_dot(a_ref[0, r, :], wo_ref[:nv, :]) + _dot(y_ref[0, r, :], wo_ref[nv:, :]) for r in rows]
    x1 = [x_ref[0, r, :] + gate1 * m for r, m in zip(rows, mix)]
    hmod = [(_rms(v) * (1.0 + scale2) + shift2).astype(BF16) for v in x1]
    fc = _Plan.ff_chunk
    acc = [None] * len(rows)
    for j in range(w1_ref.shape[1] // fc):
        u = [jnp.maximum(_dot(h, w1_ref[:, j * fc:(j + 1) * fc]), 0.0) for h in hmod]
        for r, v in enumerate(u):
            part = _dot((v * v).astype(BF16), w2_ref[j * fc:(j + 1) * fc, :])
            acc[r] = part if acc[r] is None else acc[r] + part
    for r, v, a in zip(rows, x1, acc):
        o_ref[0, r, :] = _rms(v + gate2 * a) * gf_ref[...]


def _out_proj_mlp(x, attn, conv, mod, wo, w1, w2, gf):
    b, s, d = x.shape
    tm = _Plan.mlp_tile
    tok = lambda w: pl.BlockSpec((1, tm, w), lambda bi, i: (bi, i, 0))
    return pl.pallas_call(
        _mlp_kernel,
        grid=(b, s // tm),
        in_specs=[tok(d), tok(attn.shape[2]), tok(conv.shape[2]),
                  _const_spec(mod.shape),
                  _const_spec(wo.shape), _const_spec(w1.shape), _const_spec(w2.shape), _const_spec(gf.shape)],
        out_specs=tok(d),
        out_shape=jax.ShapeDtypeStruct((b, s, d), F32),
        compiler_params=pltpu.CompilerParams(dimension_semantics=("arbitrary", "arbitrary"),
                                             vmem_limit_bytes=_Plan.vmem_limit),
        name="out_proj_mlp",
    )(x, attn, conv, mod, wo, w1, w2, gf)


def _rope_lanes(w, rows=False):
    half = QK_ROPE_DIM // 4
    x1 = np.concatenate([np.arange(half), 2 * half + np.arange(half)])
    order = np.concatenate([x1, x1 + half, x1, x1 + half])
    select = np.zeros((QK_ROPE_DIM, order.size), np.float32)
    select[order, np.arange(order.size)] = 1.0
    if rows:
        return jnp.dot(select.T, w, precision=lax.Precision.HIGHEST)
    return jnp.dot(w, select, precision=lax.Precision.HIGHEST)


def _rope_tables(rows, scale):
    half = QK_ROPE_DIM // 4
    f32 = np.float32
    pos = np.arange(rows * GRID_W)
    freqs = f32(ROPE_THETA) ** (-np.arange(0, 2 * half, 2, dtype=f32) / f32(2 * half))
    ang = np.concatenate([(pos // GRID_W).astype(f32)[:, None] * freqs,
                          (pos % GRID_W).astype(f32)[:, None] * freqs], axis=1)
    cos, sin = np.cos(ang), np.sin(ang)
    zeros = lambda w: np.zeros((pos.shape[0], w), f32)
    pad = LANES - QK_NOPE_DIM - 2 * ROPE_HALF
    cos_r = np.concatenate([cos, cos, zeros(pad)], axis=1)
    sin_r = np.concatenate([-sin, sin, zeros(pad)], axis=1)
    ones = np.ones((pos.shape[0], QK_NOPE_DIM), f32)
    cos_q = np.concatenate([ones, cos_r, cos_r, ones], axis=1) * f32(scale)
    sin_q = np.concatenate([zeros(QK_NOPE_DIM), sin_r, sin_r, zeros(QK_NOPE_DIM)], axis=1) * f32(scale)
    cos_k = np.concatenate([cos_r, cos_r], axis=1)
    sin_k = np.concatenate([sin_r, sin_r], axis=1)
    return cos_q.astype(f32), sin_q.astype(f32), cos_k, sin_k


def kernel(x, c, ctx, c_ctx, w_mod, b_mod, w_in, q_norm_g, w_uq, kv_norm_g, w_ukv,
           conv_w, w_out, w_mlp1, w_mlp2, final_norm_g):
    b, s, d = x.shape
    l = ctx.shape[1]
    depth = w_mod.shape[0]
    assert depth == 1, "single-layer configuration"
    assert s % _Plan.tok_tile == 0 and s % _Plan.q_tile == 0 and s % _Plan.kv_chunk == 0
    assert s % _Plan.mlp_tile == 0 and s % GRID_W == 0 and l % LANES == 0 and N_HEADS % HEADS_PER_STEP == 0
    cw = conv_w.shape[2]
    assert conv_w.shape[1] == CONV_K and w_in.shape[2] == MLA_IN + 3 * cw and d == N_HEADS * V_HEAD_DIM + cw

    ctx_row = b
    rows = -(-(b + 1) // SUBLANES_F32) * SUBLANES_F32
    cvec = jnp.concatenate([c, c_ctx[None, :], jnp.zeros((rows - b - 1, d), F32)], axis=0)
    mod = _adaln(cvec, w_mod[0], b_mod[0][None, :])

    wi = jnp.swapaxes(w_in[0], 0, 1)
    n_lat = Q_LORA_RANK + KV_LORA_RANK
    w_rope = _rope_lanes(wi[n_lat:MLA_IN], rows=True)
    w_mla = jnp.concatenate([wi[:n_lat], w_rope, w_rope], axis=0).astype(BF16)
    w_conv = wi[MLA_IN:].astype(BF16)
    wq = w_uq[0].reshape(Q_LORA_RANK, N_HEADS, QK_DIM)
    wq_nope, wq_rope = wq[:, :, :QK_NOPE_DIM], _rope_lanes(wq[:, :, QK_NOPE_DIM:])
    w_uq_p = jnp.stack([jnp.concatenate([wq_nope[:, 0::2], wq_rope[:, 0::2]], axis=2),
                        jnp.concatenate([wq_rope[:, 1::2], wq_nope[:, 1::2]], axis=2)], axis=2)
    w_uq_p = w_uq_p.reshape(Q_LORA_RANK, N_HEADS * HEAD_SLOT).astype(BF16)
    wkv = w_ukv[0].reshape(KV_LORA_RANK, N_HEADS, QK_NOPE_DIM + V_HEAD_DIM)
    w_ukv_p = jnp.concatenate([wkv[:, :, :QK_NOPE_DIM].reshape(KV_LORA_RANK, N_HEADS * QK_NOPE_DIM),
                               wkv[:, :, QK_NOPE_DIM:].reshape(KV_LORA_RANK, N_HEADS * V_HEAD_DIM)],
                              axis=1).astype(BF16)

    cos_q, sin_q, cos_k, sin_k = _rope_tables(s // GRID_W, ATTN_SCALE * LOG2E)
    lane = np.arange(LANES) % QK_NOPE_DIM
    ctx_cos = np.broadcast_to((lane < QK_ROPE_DIM).astype(np.float32), (l, LANES))
    ctx_sin = np.zeros((l, LANES), np.float32)

    q, k, v, conv, w1, w2 = _project_tokens(x, mod, w_mla, w_conv, q_norm_g, w_uq_p, kv_norm_g, w_ukv_p,
                                            conv_w[0], (cos_q, sin_q, cos_k, sin_k), (w_mlp1[0], w_mlp2[0]), l)
    k, v = _project_ctx(ctx, mod, ctx_row, w_mla, kv_norm_g, w_ukv_p, (ctx_cos, ctx_sin), k, v)
    attn = _attention(q, k, v)

    return _out_proj_mlp(x, attn, conv, mod, w_out[0].astype(BF16), w1, w2, final_norm_g[None, :])
```

```python
import functools
import math

import jax
import jax.numpy as jnp
import numpy as np
from jax import lax
from jax.experimental import pallas as pl
from jax.experimental.pallas import tpu as pltpu

GRID_W = 64
N_HEADS = 8
QK_NOPE_DIM = 64
QK_ROPE_DIM = 32
V_HEAD_DIM = 64
Q_LORA_RANK = 256
KV_LORA_RANK = 128
CONV_K = 3
ROPE_THETA = 10000.0
EPS = 1e-6
QK_DIM = QK_NOPE_DIM + QK_ROPE_DIM
ROPE_HALF = QK_ROPE_DIM // 2
MLA_IN = Q_LORA_RANK + KV_LORA_RANK + QK_ROPE_DIM
ATTN_SCALE = 1.0 / math.sqrt(QK_DIM)
LOG2E = math.log2(math.e)

LANES = 128
SUBLANES_F32 = 8
BF16_ROWS = 16
VMEM_BYTES_V7X = 64 * 1024 * 1024

HEAD_SLOT = LANES
HEADS_PER_STEP = 2
HALO = SUBLANES_F32

F32 = jnp.float32
BF16 = jnp.bfloat16


class _Plan:
    tok_tile = 1024
    tok_sub_tiles = 2
    tok_issue_order = (0, 0, 1, 0, 1, 0, 1, 1)
    ctx_batches_per_step = 4
    mlp_tile = 1024
    mlp_sub_tiles = 4
    q_tile = 1024
    kv_chunk = 512
    ff_chunk = 1024
    mod_cols = 3072
    vmem_limit = 56 * 1024 * 1024
    assert vmem_limit < VMEM_BYTES_V7X


def _rms(x):
    return x * lax.rsqrt(jnp.mean(x * x, axis=-1, keepdims=True) + EPS)


def _dot(a, b):
    return jnp.dot(a, b, preferred_element_type=F32)


def _dot_nt(a, b):
    return lax.dot_general(a, b, (((1,), (1,)), ((), ())), preferred_element_type=F32)


def _mod_vectors(mod_ref, row, first, count):
    d = mod_ref.shape[1] // 6
    vec = mod_ref[pl.ds(row, 1), :]
    return [vec[:, k * d:(k + 1) * d] for k in range(first, first + count)]


def _rope(t, cos_tab, sin_tab):
    return t * cos_tab + pltpu.roll(t, LANES - ROPE_HALF, 1) * sin_tab


def _adaln_kernel(c_ref, w_ref, b_ref, o_ref):
    c = c_ref[...]
    a = (c / (1.0 + jnp.exp(-c))).astype(BF16)
    o_ref[...] = _dot(a, w_ref[...].astype(BF16)) + b_ref[...]


def _adaln(cvec, w_mod, b_mod):
    rows, d = cvec.shape
    n = w_mod.shape[1]
    tn = _Plan.mod_cols
    return pl.pallas_call(
        _adaln_kernel,
        grid=(n // tn,),
        in_specs=[pl.BlockSpec((rows, d), lambda j: (0, 0)),
                  pl.BlockSpec((d, tn), lambda j: (0, j)),
                  pl.BlockSpec((1, tn), lambda j: (0, j))],
        out_specs=pl.BlockSpec((rows, tn), lambda j: (0, j)),
        out_shape=jax.ShapeDtypeStruct((rows, n), F32),
        compiler_params=pltpu.CompilerParams(dimension_semantics=("arbitrary",),
                                             vmem_limit_bytes=_Plan.vmem_limit),
        name="adaln",
    )(cvec, w_mod, b_mod)


def _kv_from_z(z_ckv, z_rope, kvg_ref, wukv_ref, cos_k, sin_k, k_ref, v_ref, rows):
    ckv = _rms(z_ckv) * kvg_ref[...]
    kv = _dot(ckv.astype(BF16), wukv_ref[...])
    k_rope = _rope(z_rope, cos_k, sin_k)
    n_nope = N_HEADS * QK_NOPE_DIM
    low = lax.broadcasted_iota(jnp.int32, k_rope.shape, 1) < QK_NOPE_DIM
    for h in range(N_HEADS):
        pair = kv[:, (h // 2) * LANES:(h // 2 + 1) * LANES]
        slot = jnp.where(low, pair, k_rope) if h % 2 == 0 else jnp.where(low, k_rope, pair)
        k_ref[0, rows, h * HEAD_SLOT:(h + 1) * HEAD_SLOT] = slot.astype(BF16)
    v_ref[0, rows, :] = kv[:, n_nope:].astype(BF16)


def _proj_sub_tile(j, n_sub, xp_ref, x_ref, xn_ref, mod_ref, wmla_ref, wconv_ref, qg_ref, wuq_ref, kvg_ref,
                   wukv_ref, cw_ref, cq_ref, sq_ref, ck_ref, sk_ref, q_ref, k_ref, v_ref, y_ref, u_scr):
    i = pl.program_id(1)
    n = pl.num_programs(1)
    sub = x_ref.shape[1] // n_sub
    cw = y_ref.shape[2]
    rows = slice(j * sub, (j + 1) * sub)
    main = slice(HALO, HALO + sub)
    c0 = Q_LORA_RANK
    c1 = c0 + KV_LORA_RANK
    c2 = c1 + LANES
    shift, scale = _mod_vectors(mod_ref, pl.program_id(0), 0, 2)
    before = xp_ref[0] if j == 0 else x_ref[0, j * sub - HALO:j * sub, :]
    after = xn_ref[0] if j == n_sub - 1 else x_ref[0, (j + 1) * sub:(j + 1) * sub + HALO, :]
    xe = jnp.concatenate([before, x_ref[0, rows, :], after], axis=0)
    hmod = (_rms(xe) * (1.0 + scale) + shift).astype(BF16)
    z_mla = _dot_nt(hmod, wmla_ref[...])[main]
    yield
    z_conv = _dot_nt(hmod, wconv_ref[cw:, :])
    yield
    cq = _rms(z_mla[:, :c0]) * qg_ref[...]
    qf = _dot(cq.astype(BF16), wuq_ref[...])

    u = z_conv[:, :cw] * z_conv[:, cw:]
    row = lax.broadcasted_iota(jnp.int32, (sub + 2 * HALO, 1), 0)
    if j == 0:
        u = jnp.where((row >= HALO) | (i > 0), u, 0.0)
    if j == n_sub - 1:
        u = jnp.where((row < HALO + sub) | (i < n - 1), u, 0.0)
    u_scr[j] = u
    w = cw_ref[...]
    y = (w[0:1] * u_scr[j, HALO - 1:HALO - 1 + sub]
         + w[1:2] * u_scr[j, HALO:HALO + sub]
         + w[2:3] * u_scr[j, HALO + 1:HALO + 1 + sub])

    for h in range(N_HEADS):
        sl = slice(h * HEAD_SLOT, (h + 1) * HEAD_SLOT)
        par = slice((h % 2) * LANES, (h % 2 + 1) * LANES)
        q_ref[0, rows, sl] = _rope(qf[:, sl], cq_ref[rows, par], sq_ref[rows, par]).astype(BF16)
    _kv_from_z(z_mla[:, c0:c1], z_mla[:, c1:c2], kvg_ref, wukv_ref, ck_ref[rows, :], sk_ref[rows, :],
               k_ref, v_ref, rows)
    yield
    gate_b = _dot_nt(hmod, wconv_ref[:cw, :])[main]
    y_ref[0, rows, :] = (gate_b * y).astype(BF16)
    yield


_PROJ_INPUT_REFS = 15


def _proj_kernel(n_cast, *refs):
    n_in = _PROJ_INPUT_REFS
    ins, cast_in = refs[:n_in], refs[n_in:n_in + n_cast]
    outs = refs[n_in + n_cast:n_in + n_cast + 4]
    cast_out, u_scr = refs[n_in + n_cast + 4:n_in + 2 * n_cast + 4], refs[n_in + 2 * n_cast + 4]
    n_sub = _Plan.tok_sub_tiles
    tiles = [_proj_sub_tile(j, n_sub, *ins, *outs, u_scr) for j in range(n_sub)]
    for j in _Plan.tok_issue_order:
        next(tiles[j])
    for src, dst in zip(cast_in, cast_out):
        dst[...] = src[...].astype(BF16)


def _ctx_kv_kernel(ctx_row, x_ref, mod_ref, win_ref, kvg_ref, wukv_ref, ck_ref, sk_ref, k_all_ref, v_all_ref,
                   k_ref, v_ref):
    del k_all_ref, v_all_ref
    nb, l, d = x_ref.shape
    shift, scale = _mod_vectors(mod_ref, ctx_row, 0, 2)
    hmod = (_rms(x_ref[...].reshape(nb * l, d)) * (1.0 + scale) + shift).astype(BF16)
    z = _dot_nt(hmod, win_ref[...])
    for g in range(nb):
        zg = z[g * l:(g + 1) * l]
        _kv_from_z(zg[:, :KV_LORA_RANK], zg[:, KV_LORA_RANK:], kvg_ref, wukv_ref, ck_ref[...], sk_ref[...],
                   k_ref.at[pl.ds(g, 1)], v_ref.at[pl.ds(g, 1)], slice(None))


def _const_spec(shape):
    return pl.BlockSpec(shape, lambda *_: (0,) * len(shape))


def _cast_block_rows(rows, steps):
    per = -(-rows // steps)
    return -(-per // BF16_ROWS) * BF16_ROWS


def _project_tokens(x, mod, w_mla, w_conv, q_g, w_uq_p, kv_g, w_ukv_p, conv_w, tabs, weights_f32, ctx_len):
    b, s, d = x.shape
    tm = _Plan.tok_tile
    cw = conv_w.shape[1]
    nh = N_HEADS * HEAD_SLOT
    nv = N_HEADS * V_HEAD_DIM
    seq_tiles = s // tm
    steps = b * seq_tiles
    halo_blocks = tm // HALO
    last_halo = s // HALO - 1

    def cast_spec(w):
        rb = _cast_block_rows(w.shape[0], steps)
        assert w.shape[0] % rb == 0
        nblk = w.shape[0] // rb
        return pl.BlockSpec((rb, w.shape[1]),
                            lambda bi, i: (jnp.minimum((bi * seq_tiles + i) * nblk // steps, nblk - 1), 0))

    cast_specs = [cast_spec(w) for w in weights_f32]
    tab_spec = lambda w: pl.BlockSpec((tm, w), lambda bi, i: (i, 0))
    tok = lambda w: pl.BlockSpec((1, tm, w), lambda bi, i: (bi, i, 0))
    return pl.pallas_call(
        functools.partial(_proj_kernel, len(weights_f32)),
        grid=(b, seq_tiles),
        in_specs=[
            pl.BlockSpec((1, HALO, d), lambda bi, i: (bi, jnp.maximum(i * halo_blocks - 1, 0), 0)),
            tok(d),
            pl.BlockSpec((1, HALO, d), lambda bi, i: (bi, jnp.minimum((i + 1) * halo_blocks, last_halo), 0)),
            _const_spec(mod.shape),
            _const_spec(w_mla.shape), _const_spec(w_conv.shape), _const_spec(q_g.shape), _const_spec(w_uq_p.shape),
            _const_spec(kv_g.shape), _const_spec(w_ukv_p.shape), _const_spec(conv_w.shape),
            tab_spec(2 * LANES), tab_spec(2 * LANES), tab_spec(LANES), tab_spec(LANES),
        ] + cast_specs,
        out_specs=[tok(nh), tok(nh), tok(nv), tok(cw)] + cast_specs,
        out_shape=[jax.ShapeDtypeStruct((b, s, nh), BF16), jax.ShapeDtypeStruct((b, s + ctx_len, nh), BF16),
                   jax.ShapeDtypeStruct((b, s + ctx_len, nv), BF16), jax.ShapeDtypeStruct((b, s, cw), BF16)]
                  + [jax.ShapeDtypeStruct(w.shape, BF16) for w in weights_f32],
        scratch_shapes=[pltpu.VMEM((_Plan.tok_sub_tiles, tm // _Plan.tok_sub_tiles + 2 * HALO, cw), F32)],
        compiler_params=pltpu.CompilerParams(dimension_semantics=("arbitrary", "arbitrary"),
                                             vmem_limit_bytes=_Plan.vmem_limit),
        name="token_proj",
    )(x, x, x, mod, w_mla, w_conv, q_g, w_uq_p, kv_g, w_ukv_p, conv_w, *tabs, *weights_f32)


def _project_ctx(ctx, mod, ctx_row, w_mla, kv_g, w_ukv_p, tabs, k_all, v_all):
    b, l, d = ctx.shape
    s = k_all.shape[1] - l
    kv_cols = 2 * LANES
    group = _Plan.ctx_batches_per_step
    assert b % group == 0 and s % l == 0
    ctx_rows = lambda w: pl.BlockSpec((group, l, w), lambda bi: (bi, s // l, 0))
    in_place = pl.BlockSpec(memory_space=pl.ANY)
    return pl.pallas_call(
        functools.partial(_ctx_kv_kernel, ctx_row),
        grid=(b // group,),
        in_specs=[
            pl.BlockSpec((group, l, d), lambda bi: (bi, 0, 0)),
            _const_spec(mod.shape),
            pl.BlockSpec((kv_cols, d), lambda bi: (Q_LORA_RANK // kv_cols, 0)),
            _const_spec(kv_g.shape), _const_spec(w_ukv_p.shape),
            _const_spec(tabs[0].shape), _const_spec(tabs[1].shape),
            in_place, in_place,
        ],
        out_specs=[ctx_rows(k_all.shape[2]), ctx_rows(v_all.shape[2])],
        out_shape=[jax.ShapeDtypeStruct(k_all.shape, k_all.dtype), jax.ShapeDtypeStruct(v_all.shape, v_all.dtype)],
        input_output_aliases={7: 0, 8: 1},
        compiler_params=pltpu.CompilerParams(dimension_semantics=("arbitrary",),
                                             vmem_limit_bytes=_Plan.vmem_limit),
        name="ctx_proj",
    )(ctx, mod, w_mla, kv_g, w_ukv_p, *tabs, k_all, v_all)


def _lane_tile_reduce(op, a):
    return functools.reduce(op, [a[:, t * LANES:(t + 1) * LANES] for t in range(a.shape[1] // LANES)])


def _key_chunks(keys):
    tk = _Plan.kv_chunk
    chunks = [(c * tk, tk) for c in range(keys // tk)]
    return chunks + ([(keys // tk * tk, keys % tk)] if keys % tk else [])


def _scores(q_ref, q_rows, k_ref, head, s_scr):
    hs = slice(head * HEAD_SLOT, (head + 1) * HEAD_SLOT)
    q = q_ref[0, q_rows, hs]
    m_run = None
    for r0, rows in _key_chunks(k_ref.shape[1]):
        s = _dot_nt(q, k_ref[0, r0:r0 + rows, hs])
        s_scr[:, r0:r0 + rows] = s
        cm = _lane_tile_reduce(jnp.maximum, s)
        m_run = cm if m_run is None else jnp.maximum(m_run, cm)
    return m_run


def _weighted_values(s_scr, m_lanes, v_ref):
    m_rep = jnp.broadcast_to(jnp.max(m_lanes, axis=-1, keepdims=True), m_lanes.shape)
    acc = None
    for r0, rows in _key_chunks(v_ref.shape[1]):
        tiles = [jnp.exp2(s_scr[:, r0 + t * LANES:r0 + (t + 1) * LANES] - m_rep) for t in range(rows // LANES)]
        rhs = jnp.concatenate([v_ref[0, r0:r0 + rows, :], jnp.ones((rows, LANES), BF16)], axis=1)
        pv = _dot(jnp.concatenate(tiles, axis=1).astype(BF16), rhs)
        acc = pv if acc is None else acc + pv
    return acc


def _head_pair_output(acc0, acc1):
    o0 = acc0[:, :LANES] / acc0[:, LANES:]
    o1 = acc1[:, :LANES] / acc1[:, LANES:]
    lane = lax.broadcasted_iota(jnp.int32, o0.shape, 1)
    return jnp.where(lane < V_HEAD_DIM, o0, o1).astype(BF16)


def _attn_kernel(q_ref, k_ref, vp_ref, vn_ref, o_ref, s0_scr, s1_scr, m1_scr, o0_scr, oa_scr):
    t = pl.program_id(0)
    tq = s0_scr.shape[0]
    rows_a, rows_b = slice(0, tq), slice(tq, 2 * tq)

    @pl.when(t == 0)
    def _():
        s1_scr[...] = jnp.zeros_like(s1_scr)
        m1_scr[...] = jnp.zeros_like(m1_scr)
        o0_scr[...] = jnp.ones_like(o0_scr)
        oa_scr[...] = jnp.zeros_like(oa_scr)

    m0 = _scores(q_ref, rows_a, k_ref, 0, s0_scr)
    acc1 = _weighted_values(s1_scr, m1_scr[...], vp_ref)
    o_ref[0, rows_a, :] = oa_scr[...]
    o_ref[0, rows_b, :] = _head_pair_output(o0_scr[...], acc1)
    m1 = _scores(q_ref, rows_a, k_ref, 1, s1_scr)
    o0_scr[...] = _weighted_values(s0_scr, m0, vn_ref)
    m0 = _scores(q_ref, rows_b, k_ref, 0, s0_scr)
    acc1 = _weighted_values(s1_scr, m1, vn_ref)
    oa_scr[...] = _head_pair_output(o0_scr[...], acc1)
    m1_scr[...] = _scores(q_ref, rows_b, k_ref, 1, s1_scr)
    o0_scr[...] = _weighted_values(s0_scr, m0, vn_ref)


def _attention(q, k, v):
    b, s, _ = q.shape
    keys = k.shape[1]
    tq = _Plan.q_tile
    rows = 2 * tq
    assert s % rows == 0
    groups = N_HEADS // HEADS_PER_STEP
    n_q = s // rows
    tiles = b * groups * n_q
    pair_w = HEADS_PER_STEP * HEAD_SLOT
    o_w = HEADS_PER_STEP * V_HEAD_DIM

    def decode(u):
        return u // (groups * n_q), (u // n_q) % groups, u % n_q

    def cur(t):
        return decode(jnp.minimum(t, tiles - 1))

    def prev(t):
        return decode(jnp.maximum(t - 1, 0))

    def keys_of(which, width):
        return pl.BlockSpec((1, keys, width), lambda t: (which(t)[0], 0, which(t)[1]))

    def rows_of(which, width):
        return pl.BlockSpec((1, rows, width), lambda t: (which(t)[0], which(t)[2], which(t)[1]))

    return pl.pallas_call(
        _attn_kernel,
        grid=(tiles + 1,),
        in_specs=[rows_of(cur, pair_w), keys_of(cur, pair_w), keys_of(prev, o_w), keys_of(cur, o_w)],
        out_specs=rows_of(prev, o_w),
        out_shape=jax.ShapeDtypeStruct((b, s, N_HEADS * V_HEAD_DIM), BF16),
        scratch_shapes=[pltpu.VMEM((tq, keys), F32), pltpu.VMEM((tq, keys), F32),
                        pltpu.VMEM((tq, LANES), F32),
                        pltpu.VMEM((tq, 2 * LANES), F32),
                        pltpu.VMEM((tq, o_w), BF16)],
        compiler_params=pltpu.CompilerParams(dimension_semantics=("arbitrary",),
                                             vmem_limit_bytes=_Plan.vmem_limit),
        name="attention",
    )(q, k, v, v)


def _mlp_kernel(x_ref, a_ref, y_ref, mod_ref, wo_ref, w1_ref, w2_ref, gf_ref, o_ref):
    gate1, shift2, scale2, gate2 = _mod_vectors(mod_ref, pl.program_id(0), 2, 4)
    sub = x_ref.shape[1] // _Plan.mlp_sub_tiles
    rows = [slice(r * sub, (r + 1) * sub) for r in range(_Plan.mlp_sub_tiles)]
    nv = a_ref.shape[2]
    mix = [_dot(a_ref[0, r, :], wo_ref[:nv, :]) + _dot(y_ref[0, r, :], wo_ref[nv:, :]) for r in rows]
    x1 = [x_ref[0, r, :] + gate1 * m for r, m in zip(rows, mix)]
    hmod = [(_rms(v) * (1.0 + scale2) + shift2).astype(BF16) for v in x1]
    fc = _Plan.ff_chunk
    acc = [None] * len(rows)
    for j in range(w1_ref.shape[1] // fc):
        u = [jnp.maximum(_dot(h, w1_ref[:, j * fc:(j + 1) * fc]), 0.0) for h in hmod]
        for r, v in enumerate(u):
            part = _dot((v * v).astype(BF16), w2_ref[j * fc:(j + 1) * fc, :])
            acc[r] = part if acc[r] is None else acc[r] + part
    for r, v, a in zip(rows, x1, acc):
        o_ref[0, r, :] = _rms(v + gate2 * a) * gf_ref[...]


def _out_proj_mlp(x, attn, conv, mod, wo, w1, w2, gf):
    b, s, d = x.shape
    tm = _Plan.mlp_tile
    tok = lambda w: pl.BlockSpec((1, tm, w), lambda bi, i: (bi, i, 0))
    return pl.pallas_call(
        _mlp_kernel,
        grid=(b, s // tm),
        in_specs=[tok(d), tok(attn.shape[2]), tok(conv.shape[2]),
                  _const_spec(mod.shape),
                  _const_spec(wo.shape), _const_spec(w1.shape), _const_spec(w2.shape), _const_spec(gf.shape)],
        out_specs=tok(d),
        out_shape=jax.ShapeDtypeStruct((b, s, d), F32),
        compiler_params=pltpu.CompilerParams(dimension_semantics=("arbitrary", "arbitrary"),
                                             vmem_limit_bytes=_Plan.vmem_limit),
        name="out_proj_mlp",
    )(x, attn, conv, mod, wo, w1, w2, gf)


def _rope_lanes(w, rows=False):
    half = QK_ROPE_DIM // 4
    x1 = np.concatenate([np.arange(half), 2 * half + np.arange(half)])
    order = np.concatenate([x1, x1 + half, x1, x1 + half])
    select = np.zeros((QK_ROPE_DIM, order.size), np.float32)
    select[order, np.arange(order.size)] = 1.0
    if rows:
        return jnp.dot(select.T, w, precision=lax.Precision.HIGHEST)
    return jnp.dot(w, select, precision=lax.Precision.HIGHEST)


def _rope_tables(rows, scale):
    half = QK_ROPE_DIM // 4
    f32 = np.float32
    pos = np.arange(rows * GRID_W)
    freqs = f32(ROPE_THETA) ** (-np.arange(0, 2 * half, 2, dtype=f32) / f32(2 * half))
    ang = np.concatenate([(pos // GRID_W).astype(f32)[:, None] * freqs,
                          (pos % GRID_W).astype(f32)[:, None] * freqs], axis=1)
    cos, sin = np.cos(ang), np.sin(ang)
    zeros = lambda w: np.zeros((pos.shape[0], w), f32)
    pad = LANES - QK_NOPE_DIM - 2 * ROPE_HALF
    cos_r = np.concatenate([cos, cos, zeros(pad)], axis=1)
    sin_r = np.concatenate([-sin, sin, zeros(pad)], axis=1)
    ones = np.ones((pos.shape[0], QK_NOPE_DIM), f32)
    cos_q = np.concatenate([ones, cos_r, cos_r, ones], axis=1) * f32(scale)
    sin_q = np.concatenate([zeros(QK_NOPE_DIM), sin_r, sin_r, zeros(QK_NOPE_DIM)], axis=1) * f32(scale)
    cos_k = np.concatenate([cos_r, cos_r], axis=1)
    sin_k = np.concatenate([sin_r, sin_r], axis=1)
    return cos_q.astype(f32), sin_q.astype(f32), cos_k, sin_k


def kernel(x, c, ctx, c_ctx, w_mod, b_mod, w_in, q_norm_g, w_uq, kv_norm_g, w_ukv,
           conv_w, w_out, w_mlp1, w_mlp2, final_norm_g):
    b, s, d = x.shape
    l = ctx.shape[1]
    depth = w_mod.shape[0]
    assert depth == 1, "single-layer configuration"
    assert s % _Plan.tok_tile == 0 and s % _Plan.q_tile == 0 and s % _Plan.kv_chunk == 0
    assert s % _Plan.mlp_tile == 0 and s % GRID_W == 0 and l % LANES == 0 and N_HEADS % HEADS_PER_STEP == 0
    cw = conv_w.shape[2]
    assert conv_w.shape[1] == CONV_K and w_in.shape[2] == MLA_IN + 3 * cw and d == N_HEADS * V_HEAD_DIM + cw

    ctx_row = b
    rows = -(-(b + 1) // SUBLANES_F32) * SUBLANES_F32
    cvec = jnp.concatenate([c, c_ctx[None, :], jnp.zeros((rows - b - 1, d), F32)], axis=0)
    mod = _adaln(cvec, w_mod[0], b_mod[0][None, :])

    wi = jnp.swapaxes(w_in[0], 0, 1)
    n_lat = Q_LORA_RANK + KV_LORA_RANK
    w_rope = _rope_lanes(wi[n_lat:MLA_IN], rows=True)
    w_mla = jnp.concatenate([wi[:n_lat], w_rope, w_rope], axis=0).astype(BF16)
    w_conv = wi[MLA_IN:].astype(BF16)
    wq = w_uq[0].reshape(Q_LORA_RANK, N_HEADS, QK_DIM)
    wq_nope, wq_rope = wq[:, :, :QK_NOPE_DIM], _rope_lanes(wq[:, :, QK_NOPE_DIM:])
    w_uq_p = jnp.stack([jnp.concatenate([wq_nope[:, 0::2], wq_rope[:, 0::2]], axis=2),
                        jnp.concatenate([wq_rope[:, 1::2], wq_nope[:, 1::2]], axis=2)], axis=2)
    w_uq_p = w_uq_p.reshape(Q_LORA_RANK, N_HEADS * HEAD_SLOT).astype(BF16)
    wkv = w_ukv[0].reshape(KV_LORA_RANK, N_HEADS, QK_NOPE_DIM + V_HEAD_DIM)
    w_ukv_p = jnp.concatenate([wkv[:, :, :QK_NOPE_DIM].reshape(KV_LORA_RANK, N_HEADS * QK_NOPE_DIM),
                               wkv[:, :, QK_NOPE_DIM:].reshape(KV_LORA_RANK, N_HEADS * V_HEAD_DIM)],
                              axis=1).astype(BF16)

    cos_q, sin_q, cos_k, sin_k = _rope_tables(s // GRID_W, ATTN_SCALE * LOG2E)
    lane = np.arange(LANES) % QK_NOPE_DIM
    ctx_cos = np.broadcast_to((lane < QK_ROPE_DIM).astype(np.float32), (l, LANES))
    ctx_sin = np.zeros((l, LANES), np.float32)

    q, k, v, conv, w1, w2 = _project_tokens(x, mod, w_mla, w_conv, q_norm_g, w_uq_p, kv_norm_g, w_ukv_p,
                                            conv_w[0], (cos_q, sin_q, cos_k, sin_k), (w_mlp1[0], w_mlp2[0]), l)
    k, v = _project_ctx(ctx, mod, ctx_row, w_mla, kv_norm_g, w_ukv_p, (ctx_cos, ctx_sin), k, v)
    attn = _attention(q, k, v)

    return _out_proj_mlp(x, attn, conv, mod, w_out[0].astype(BF16), w1, w2, final_norm_g[None, :])
```
